```python
import jax
import jax.numpy as jnp
from jax import lax
import numpy as np

D_MODEL = 4096
BATCH = 1
SEQ = 8192
DEPTH = 1
DEC_BATCH = 32
DEC_SEQ = 4
PAST_LEN = 8192
PAGE_SIZE = 128

HEAD_DIM = 128
FOX_HEADS = 16
FOX_KV_HEADS = 4
NSA_HEADS = 16
NSA_KV_HEADS = 4
FOX_WIDTH = FOX_HEADS * HEAD_DIM
NSA_WIDTH = NSA_HEADS * HEAD_DIM
N_BRANCH = 2
ROPE_THETA = 500000.0
ROPE_DIM = HEAD_DIM // 4
CMP_LEN = 32
CMP_STRIDE = 16
CMP_HIDDEN = 2 * HEAD_DIM
SEL_LEN = 64
N_SEL = 16
WINDOW = 512
N_EXPERTS = 64
N_EXPERT_GROUPS = 8
TOPK_GROUPS = 4
TOP_K = 8
EXPERT_HIDDEN = 1024
SHARED_HIDDEN = 1024
ROUTED_SCALE = 2.5
FORGET_BIAS_INIT = 2.0
Q_BLOCK = 128
MOE_BLOCK = 128
EPS = 1e-6
NEG = -1e30
BIG = 1e30

kernel_name = 'hybrid_fox_nsa_moe_adaln_decode_step'


def _in_sizes():
    kvf = FOX_KV_HEADS * HEAD_DIM
    kvn = NSA_KV_HEADS * HEAD_DIM
    return [FOX_WIDTH, kvf, kvf, FOX_HEADS, NSA_WIDTH, kvn, kvn, kvn, kvn, kvn, kvn,
            3 * NSA_HEADS, N_BRANCH * D_MODEL]


def rms_norm(x, g):
    xf = x.astype(jnp.float32)
    y = xf * lax.rsqrt(jnp.mean(xf * xf, axis=-1, keepdims=True) + EPS)
    return (y * g.astype(jnp.float32)).astype(x.dtype)


def rope(x, pos):
    half = ROPE_DIM // 2
    inv = ROPE_THETA ** (-2.0 * jnp.arange(half, dtype=jnp.float32) / ROPE_DIM)
    ang = pos.astype(jnp.float32)[:, None] * inv[None, :]
    cos = jnp.cos(ang)[None, :, None, :]
    sin = jnp.sin(ang)[None, :, None, :]
    xr = x[..., :ROPE_DIM].astype(jnp.float32)
    x1, x2 = xr[..., :half], xr[..., half:]
    rot = jnp.concatenate([x1 * cos - x2 * sin, x2 * cos + x1 * sin], axis=-1).astype(x.dtype)
    return jnp.concatenate([rot, x[..., ROPE_DIM:]], axis=-1)


def q_blocking(n_q):
    qb = min(Q_BLOCK, n_q)
    return qb, n_q // qb


def swiglu(x, wg, wu, wd):
    return jnp.dot(jax.nn.silu(jnp.dot(x, wg)) * jnp.dot(x, wu), wd)


def gather_pages(cache, page_table):
    g = cache[page_table]
    return g.reshape((page_table.shape[0], page_table.shape[1] * cache.shape[1]) + cache.shape[2:])


def fox_attention(q, k, v, F, q0):
    B, Tq, H, _ = q.shape
    Lk, G = k.shape[1], k.shape[2]
    R = H // G
    qb, nb = q_blocking(Tq)
    scale = HEAD_DIM ** -0.5
    qg = q.reshape(B, nb, qb, G, R, HEAD_DIM)
    Fq = lax.dynamic_slice_in_dim(F, q0, Tq, axis=1).reshape(B, nb, qb, G, R)
    Fk = jnp.transpose(F.reshape(B, Lk, G, R), (0, 2, 3, 1))[:, :, :, None, :]
    kidx = jnp.arange(Lk)

    def block(i):
        qidx = q0 + i * qb + jnp.arange(qb)
        s = jnp.einsum('bqgrd,bkgd->bgrqk', qg[:, i], k, preferred_element_type=jnp.float32) * scale
        decay = jnp.transpose(Fq[:, i], (0, 2, 3, 1))[..., None] - Fk
        mask = kidx[None, :] <= qidx[:, None]
        p = jax.nn.softmax(jnp.where(mask, s + decay, NEG), axis=-1)
        return jnp.einsum('bgrqk,bkgd->bqgrd', p.astype(v.dtype), v)

    o = lax.map(block, jnp.arange(nb))
    return jnp.moveaxis(o, 0, 1).reshape(B, Tq, H, HEAD_DIM)


def compress(x, pe, w1, w2):
    B, L, G, _ = x.shape
    m = CMP_LEN // CMP_STRIDE
    n_chunks = max(-(-L // CMP_STRIDE), m)
    xp = jnp.pad(x, ((0, 0), (0, n_chunks * CMP_STRIDE - L), (0, 0), (0, 0)))
    chunks = xp.reshape(B, n_chunks, CMP_STRIDE, G, HEAD_DIM)
    nc = n_chunks - m + 1
    w1r = w1.reshape(m, CMP_STRIDE, HEAD_DIM, CMP_HIDDEN)
    per = pe.reshape(m, CMP_STRIDE, HEAD_DIM)
    pre = 0.0
    for j in range(m):
        pre = pre + jnp.einsum('bnlgd,ldh->bngh', chunks[:, j:j + nc] + per[j][None, None, :, None, :], w1r[j])
    out = jnp.dot(jax.nn.gelu(pre), w2)
    c_end = jnp.arange(nc) * CMP_STRIDE + CMP_LEN - 1
    return out, c_end


def nsa_compressed_selected(q, ck, cv, c_end, sk, sv, q0):
    B, Tq, H, _ = q.shape
    G = ck.shape[2]
    R = H // G
    NS = sk.shape[1] // SEL_LEN
    n_sel = min(N_SEL, NS)
    qb, nb = q_blocking(Tq)
    scale = HEAD_DIM ** -0.5
    qg = q.reshape(B, nb, qb, G, R, HEAD_DIM)
    c_start = c_end - (CMP_LEN - 1)
    s_first = jnp.arange(NS) * SEL_LEN
    overlap = ((c_start[:, None] < s_first[None, :] + SEL_LEN) & (c_end[:, None] >= s_first[None, :])).astype(jnp.float32)
    skb = jnp.transpose(sk.reshape(B, NS, SEL_LEN, G, HEAD_DIM), (0, 3, 1, 2, 4))
    svb = jnp.transpose(sv.reshape(B, NS, SEL_LEN, G, HEAD_DIM), (0, 3, 1, 2, 4))
    b_ix = jnp.arange(B)[:, None, None, None]
    g_ix = jnp.arange(G)[None, None, :, None]
    blk_ids = jnp.arange(NS)

    def block(i):
        qi = qg[:, i]
        qidx = q0 + i * qb + jnp.arange(qb)
        s = jnp.einsum('bqgrd,bcgd->bgrqc', qi, ck, preferred_element_type=jnp.float32) * scale
        cmask = c_end[None, :] <= qidx[:, None]
        p = jax.nn.softmax(jnp.where(cmask, s, NEG), axis=-1) * cmask
        o_cmp = jnp.einsum('bgrqc,bcgd->bqgrd', p.astype(cv.dtype), cv)
        imp = jnp.einsum('bgrqc,cs->bqgs', p, overlap)
        cur = qidx // SEL_LEN
        valid = (blk_ids[None, :] * SEL_LEN <= qidx[:, None])[None, :, None, :]
        forced = ((blk_ids[None, :] == 0) | (blk_ids[None, :] == cur[:, None]) | (blk_ids[None, :] == cur[:, None] - 1))[None, :, None, :]
        score = jnp.where(valid & forced, BIG, jnp.where(valid, imp, NEG))
        top_s, top_j = lax.top_k(score, n_sel)
        kg = skb[b_ix, g_ix, top_j]
        vg = svb[b_ix, g_ix, top_j]
        kpos = top_j[..., None] * SEL_LEN + jnp.arange(SEL_LEN)
        smask = (top_s > 0.5 * NEG)[..., None] & (kpos <= qidx[None, :, None, None, None])
        ss = jnp.einsum('bqgrd,bqgnld->bqgrnl', qi, kg, preferred_element_type=jnp.float32) * scale
        ss = jnp.where(smask[:, :, :, None], ss, NEG)
        ps = jax.nn.softmax(ss.reshape(B, qb, G, R, n_sel * SEL_LEN), axis=-1).reshape(ss.shape)
        o_sel = jnp.einsum('bqgrnl,bqgnld->bqgrd', ps.astype(vg.dtype), vg)
        return o_cmp, o_sel

    o_cmp, o_sel = lax.map(block, jnp.arange(nb))
    o_cmp = jnp.moveaxis(o_cmp, 0, 1).reshape(B, Tq, H, HEAD_DIM)
    o_sel = jnp.moveaxis(o_sel, 0, 1).reshape(B, Tq, H, HEAD_DIM)
    return o_cmp, o_sel


def window_attention(q, k, v, kpos, qpos, q0):
    B, Tq, H, _ = q.shape
    G = k.shape[2]
    R = H // G
    qb, nb = q_blocking(Tq)
    scale = HEAD_DIM ** -0.5
    pad = jnp.zeros((B, WINDOW) + k.shape[2:], k.dtype)
    kp = jnp.concatenate([pad, k], axis=1)
    vp = jnp.concatenate([pad, v.astype(k.dtype)], axis=1)
    kposp = jnp.concatenate([jnp.full((WINDOW,), -(2 ** 30), jnp.int32), kpos.astype(jnp.int32)])
    span = WINDOW + qb
    qg = q.reshape(B, nb, qb, G, R, HEAD_DIM)

    def block(i):
        st = q0 + i * qb
        kb = lax.dynamic_slice_in_dim(kp, st, span, axis=1)
        vb = lax.dynamic_slice_in_dim(vp, st, span, axis=1)
        pb = lax.dynamic_slice_in_dim(kposp, st, span)
        qp = lax.dynamic_slice_in_dim(qpos, i * qb, qb)
        mask = (pb[None, :] <= qp[:, None]) & (qp[:, None] - pb[None, :] < WINDOW)
        s = jnp.einsum('bqgrd,bkgd->bgrqk', qg[:, i], kb, preferred_element_type=jnp.float32) * scale
        p = jax.nn.softmax(jnp.where(mask, s, NEG), axis=-1)
        return jnp.einsum('bgrqk,bkgd->bqgrd', p.astype(vb.dtype), vb)

    o = lax.map(block, jnp.arange(nb))
    return jnp.moveaxis(o, 0, 1).reshape(B, Tq, H, HEAD_DIM)


def moe(h, w_router, router_bias, w_exp_gate, w_exp_up, w_exp_down, w_sh_gate, w_sh_up, w_sh_down):
    B, T, D = h.shape
    N = B * T
    xt = h.reshape(N, D)
    scores = jax.nn.sigmoid(jnp.dot(xt, w_router, preferred_element_type=jnp.float32))
    biased = scores + router_bias.astype(jnp.float32)
    per_group = N_EXPERTS // N_EXPERT_GROUPS
    gscore = jnp.sum(lax.top_k(biased.reshape(N, N_EXPERT_GROUPS, per_group), 2)[0], axis=-1)
    gidx = lax.top_k(gscore, TOPK_GROUPS)[1]
    gmask = jnp.any(gidx[:, :, None] == jnp.arange(N_EXPERT_GROUPS)[None, None, :], axis=1)
    emask = jnp.repeat(gmask, per_group, axis=1)
    eidx = lax.top_k(jnp.where(emask, biased, NEG), TOP_K)[1]
    gw = jnp.take_along_axis(scores, eidx, axis=1)
    gw = gw / jnp.sum(gw, axis=-1, keepdims=True) * ROUTED_SCALE
    A = N * TOP_K
    flat_e = eidx.reshape(A)
    flat_tok = jnp.repeat(jnp.arange(N, dtype=jnp.int32), TOP_K)
    flat_w = gw.reshape(A)
    order = jnp.argsort(flat_e)
    se = flat_e[order]
    counts = jnp.bincount(flat_e, length=N_EXPERTS)
    padded = (counts + MOE_BLOCK - 1) // MOE_BLOCK * MOE_BLOCK
    start = jnp.cumsum(counts) - counts
    pend = jnp.cumsum(padded)
    pstart = pend - padded
    dest = pstart[se] + jnp.arange(A) - start[se]
    n_slots = -(-A // MOE_BLOCK) * MOE_BLOCK + N_EXPERTS * MOE_BLOCK
    n_blocks = n_slots // MOE_BLOCK
    slot_tok = jnp.full((n_slots,), N, jnp.int32).at[dest].set(flat_tok[order])
    slot_w = jnp.zeros((n_slots,), jnp.float32).at[dest].set(flat_w[order])
    block_e = jnp.minimum(jnp.searchsorted(pend, jnp.arange(n_blocks) * MOE_BLOCK, side='right'), N_EXPERTS - 1)
    xpad = jnp.concatenate([xt, jnp.zeros((1, D), xt.dtype)], axis=0)

    def run(args):
        toks, e = args
        return swiglu(xpad[toks], w_exp_gate[e], w_exp_up[e], w_exp_down[e])

    yb = lax.map(run, (slot_tok.reshape(n_blocks, MOE_BLOCK), block_e))
    contrib = yb.reshape(n_slots, D) * slot_w[:, None].astype(yb.dtype)
    routed = jax.ops.segment_sum(contrib, slot_tok, num_segments=N + 1)[:N]
    shared = swiglu(xt, w_sh_gate, w_sh_up, w_sh_down)
    return (routed + shared).reshape(B, T, D)


def layer(x, c, past, P):
    B, T, _ = x.shape
    pos0 = 0 if past is None else past['fox_k'].shape[1]
    pos = pos0 + jnp.arange(T, dtype=jnp.int32)
    mod = jnp.dot(jax.nn.silu(c), P['w_ada']) + P['b_ada']
    sh1, sc1, ga1, sh2, sc2, ga2 = jnp.split(mod[:, None, :], 6, axis=-1)
    h = rms_norm(x, P['g_mix']) * (1 + sc1) + sh1
    proj = jnp.dot(h, P['w_in'])
    cuts = [int(v) for v in np.cumsum(_in_sizes())[:-1]]
    qf, kf, vf, fl, qn, kc, vc, ks, vs, kw, vw, bg, mg = jnp.split(proj, cuts, axis=-1)
    heads = lambda t, n: t.reshape(B, T, n, HEAD_DIM)
    qf = rms_norm(heads(qf, FOX_HEADS), P['g_q_fox'])
    kf = rms_norm(heads(kf, FOX_KV_HEADS), P['g_k_fox'])
    vf = heads(vf, FOX_KV_HEADS)
    logf = jax.nn.log_sigmoid((fl + P['b_forget']).astype(jnp.float32))
    qn = rope(rms_norm(heads(qn, NSA_HEADS), P['g_q_nsa']), pos)
    kc = rope(heads(kc, NSA_KV_HEADS), pos)
    vc = heads(vc, NSA_KV_HEADS)
    ks = rope(rms_norm(heads(ks, NSA_KV_HEADS), P['g_k_sel']), pos)
    vs = heads(vs, NSA_KV_HEADS)
    kw = rope(rms_norm(heads(kw, NSA_KV_HEADS), P['g_k_win']), pos)
    vw = heads(vw, NSA_KV_HEADS)
    bgate = jax.nn.sigmoid(bg.reshape(B, T, 3, NSA_HEADS))[..., None]
    mgate = jax.nn.sigmoid(mg.reshape(B, T, N_BRANCH, D_MODEL))
    if past is None:
        c_kf, c_vf, c_logf, c_kc, c_vc, c_ks, c_vs, c_kw, c_vw = kf, vf, logf, kc, vc, ks, vs, kw, vw
        w_len = 0
    else:
        cat = lambda a, b: jnp.concatenate([a, b.astype(a.dtype)], axis=1)
        c_kf = cat(past['fox_k'], kf)
        c_vf = cat(past['fox_v'], vf)
        c_logf = jnp.concatenate([past['fox_logf'].astype(jnp.float32), logf], axis=1)
        c_kc = cat(past['cmp_k'], kc)
        c_vc = cat(past['cmp_v'], vc)
        c_ks = cat(past['sel_k'], ks)
        c_vs = cat(past['sel_v'], vs)
        c_kw = cat(past['win_k'], kw)
        c_vw = cat(past['win_v'], vw)
        w_len = past['win_k'].shape[1]
    F = jnp.cumsum(c_logf, axis=1)
    o_fox = fox_attention(qf, c_kf, c_vf, F, pos0)
    ck, c_end = compress(c_kc, P['cmp_pe_k'], P['w_cmp_k1'], P['w_cmp_k2'])
    ck = rms_norm(ck, P['g_k_cmp'])
    cv, _ = compress(c_vc, P['cmp_pe_v'], P['w_cmp_v1'], P['w_cmp_v2'])
    L = c_ks.shape[1]
    padw = ((0, 0), (0, -(-L // SEL_LEN) * SEL_LEN - L), (0, 0), (0, 0))
    o_cmp, o_sel = nsa_compressed_selected(qn, ck, cv, c_end, jnp.pad(c_ks, padw), jnp.pad(c_vs, padw), pos0)
    kpos_w = pos0 - w_len + jnp.arange(w_len + T, dtype=jnp.int32)
    o_win = window_attention(qn, c_kw, c_vw, kpos_w, pos, w_len)
    o_nsa = bgate[:, :, 0] * o_cmp + bgate[:, :, 1] * o_sel + bgate[:, :, 2] * o_win
    branches = jnp.stack([o_fox.reshape(B, T, FOX_WIDTH), o_nsa.reshape(B, T, NSA_WIDTH)], axis=2)
    merged = jnp.sum(mgate * jnp.einsum('btnc,ncd->btnd', branches, P['w_branch']), axis=2)
    x = x + ga1 * jnp.dot(merged, P['w_out'])
    h2 = rms_norm(x, P['g_ffn']) * (1 + sc2) + sh2
    x = x + ga2 * moe(h2, P['w_router'], P['router_bias'], P['w_exp_gate'], P['w_exp_up'], P['w_exp_down'],
                      P['w_sh_gate'], P['w_sh_up'], P['w_sh_down'])
    wl = min(WINDOW, c_kw.shape[1])
    new_state = (kf, vf, logf, kc, vc, ks, vs, c_kw[:, -wl:], c_vw[:, -wl:])
    return x, new_state


def setup_inputs(seed: int = 0) -> dict:
    key = jax.random.key(seed)
    keys = jax.random.split(key, 48)

    def nrm(i, shape, scale=1.0):
        return jax.random.normal(keys[i], shape, jnp.float32) * scale

    d = D_MODEL
    n_pages = PAST_LEN // PAGE_SIZE
    n_used = DEC_BATCH * n_pages
    n_pool = n_used + max(1, n_used // 4)
    wb = min(WINDOW, PAST_LEN)
    kvf = (n_pool, PAGE_SIZE, FOX_KV_HEADS, HEAD_DIM)
    kvn = (n_pool, PAGE_SIZE, NSA_KV_HEADS, HEAD_DIM)
    in_width = sum(_in_sizes())
    page_table = jax.random.permutation(keys[0], n_pool)[:n_used].reshape(DEC_BATCH, n_pages).astype(jnp.int32)
    return {
        'x_prompt': nrm(1, (BATCH, SEQ, d)),
        'x_sample': nrm(2, (DEC_BATCH, DEC_SEQ, d)),
        'c_prompt': nrm(3, (BATCH, d)),
        'c_sample': nrm(4, (DEC_BATCH, d)),
        'cache_fox_k': nrm(5, kvf),
        'cache_fox_v': nrm(6, kvf),
        'cache_fox_logf': jax.nn.log_sigmoid(FORGET_BIAS_INIT + nrm(7, (n_pool, PAGE_SIZE, FOX_HEADS))),
        'cache_cmp_k': nrm(8, kvn),
        'cache_cmp_v': nrm(9, kvn),
        'cache_sel_k': nrm(10, kvn),
        'cache_sel_v': nrm(11, kvn),
        'state_win_k': nrm(12, (DEC_BATCH, wb, NSA_KV_HEADS, HEAD_DIM)),
        'state_win_v': nrm(13, (DEC_BATCH, wb, NSA_KV_HEADS, HEAD_DIM)),
        'page_table': page_table,
        'w_ada': nrm(14, (d, 6 * d), 0.5 * d ** -0.5),
        'b_ada': nrm(15, (6 * d,), 0.01),
        'g_mix': 1.0 + nrm(16, (d,), 0.02),
        'g_ffn': 1.0 + nrm(17, (d,), 0.02),
        'w_in': nrm(18, (d, in_width), d ** -0.5),
        'b_forget': FORGET_BIAS_INIT + nrm(19, (FOX_HEADS,), 0.1),
        'g_q_fox': 1.0 + nrm(20, (HEAD_DIM,), 0.02),
        'g_k_fox': 1.0 + nrm(21, (HEAD_DIM,), 0.02),
        'g_q_nsa': 1.0 + nrm(22, (HEAD_DIM,), 0.02),
        'g_k_cmp': 1.0 + nrm(23, (HEAD_DIM,), 0.02),
        'g_k_sel': 1.0 + nrm(24, (HEAD_DIM,), 0.02),
        'g_k_win': 1.0 + nrm(25, (HEAD_DIM,), 0.02),
        'cmp_pe_k': nrm(26, (CMP_LEN, HEAD_DIM), 0.1),
        'cmp_pe_v': nrm(27, (CMP_LEN, HEAD_DIM), 0.1),
        'w_cmp_k1': nrm(28, (CMP_LEN * HEAD_DIM, CMP_HIDDEN), (CMP_LEN * HEAD_DIM) ** -0.5),
        'w_cmp_k2': nrm(29, (CMP_HIDDEN, HEAD_DIM), CMP_HIDDEN ** -0.5),
        'w_cmp_v1': nrm(30, (CMP_LEN * HEAD_DIM, CMP_HIDDEN), (CMP_LEN * HEAD_DIM) ** -0.5),
        'w_cmp_v2': nrm(31, (CMP_HIDDEN, HEAD_DIM), CMP_HIDDEN ** -0.5),
        'w_branch': nrm(32, (N_BRANCH, FOX_WIDTH, d), FOX_WIDTH ** -0.5),
        'w_out': nrm(33, (d, d), d ** -0.5),
        'w_router': nrm(34, (d, N_EXPERTS), d ** -0.5),
        'router_bias': nrm(35, (N_EXPERTS,), 0.01),
        'w_exp_gate': nrm(36, (N_EXPERTS, d, EXPERT_HIDDEN), d ** -0.5),
        'w_exp_up': nrm(37, (N_EXPERTS, d, EXPERT_HIDDEN), d ** -0.5),
        'w_exp_down': nrm(38, (N_EXPERTS, EXPERT_HIDDEN, d), EXPERT_HIDDEN ** -0.5),
        'w_sh_gate': nrm(39, (d, SHARED_HIDDEN), d ** -0.5),
        'w_sh_up': nrm(40, (d, SHARED_HIDDEN), d ** -0.5),
        'w_sh_down': nrm(41, (SHARED_HIDDEN, d), SHARED_HIDDEN ** -0.5),
    }


def reference(x_prompt, x_sample, c_prompt, c_sample, cache_fox_k, cache_fox_v, cache_fox_logf,
              cache_cmp_k, cache_cmp_v, cache_sel_k, cache_sel_v, state_win_k, state_win_v, page_table,
              w_ada, b_ada, g_mix, g_ffn, w_in, b_forget, g_q_fox, g_k_fox, g_q_nsa, g_k_cmp, g_k_sel,
              g_k_win, cmp_pe_k, cmp_pe_v, w_cmp_k1, w_cmp_k2, w_cmp_v1, w_cmp_v2, w_branch, w_out,
              w_router, router_bias, w_exp_gate, w_exp_up, w_exp_down, w_sh_gate, w_sh_up, w_sh_down):
    P = {
        'w_ada': w_ada, 'b_ada': b_ada, 'g_mix': g_mix, 'g_ffn': g_ffn, 'w_in': w_in,
        'b_forget': b_forget, 'g_q_fox': g_q_fox, 'g_k_fox': g_k_fox, 'g_q_nsa': g_q_nsa,
        'g_k_cmp': g_k_cmp, 'g_k_sel': g_k_sel, 'g_k_win': g_k_win, 'cmp_pe_k': cmp_pe_k,
        'cmp_pe_v': cmp_pe_v, 'w_cmp_k1': w_cmp_k1, 'w_cmp_k2': w_cmp_k2, 'w_cmp_v1': w_cmp_v1,
        'w_cmp_v2': w_cmp_v2, 'w_branch': w_branch, 'w_out': w_out, 'w_router': w_router,
        'router_bias': router_bias, 'w_exp_gate': w_exp_gate, 'w_exp_up': w_exp_up,
        'w_exp_down': w_exp_down, 'w_sh_gate': w_sh_gate, 'w_sh_up': w_sh_up, 'w_sh_down': w_sh_down,
    }
    past = {
        'fox_k': gather_pages(cache_fox_k, page_table),
        'fox_v': gather_pages(cache_fox_v, page_table),
        'fox_logf': gather_pages(cache_fox_logf, page_table),
        'cmp_k': gather_pages(cache_cmp_k, page_table),
        'cmp_v': gather_pages(cache_cmp_v, page_table),
        'sel_k': gather_pages(cache_sel_k, page_table),
        'sel_v': gather_pages(cache_sel_v, page_table),
        'win_k': state_win_k,
        'win_v': state_win_v,
    }
    y_prompt, y_sample = x_prompt, x_sample
    for _ in range(DEPTH):
        y_prompt, (pfk, pfv, pfl, pck, pcv, psk, psv, pwk, pwv) = layer(y_prompt, c_prompt, None, P)
        y_sample, (sfk, sfv, sfl, sck, scv, ssk, ssv, swk, swv) = layer(y_sample, c_sample, past, P)
    return (y_prompt, y_sample, pfk, pfv, pfl, pck, pcv, psk, psv, pwk, pwv,
            sfk, sfv, sfl, sck, scv, ssk, ssv, swk, swv)
```

```python
import functools

import jax
import jax.numpy as jnp
from jax import lax
from jax.experimental import pallas as pl
from jax.experimental.pallas import tpu as pltpu

F32 = jnp.float32
BF16 = jnp.bfloat16
HIGHEST = lax.Precision.HIGHEST

HEAD_DIM = 128
FOX_HEADS = 16
FOX_KV_HEADS = 4
NSA_HEADS = 16
NSA_KV_HEADS = 4
N_BRANCH = 2
ROPE_THETA = 500000.0
ROPE_DIM = HEAD_DIM // 4
CMP_LEN = 32
CMP_STRIDE = 16
SEL_LEN = 64
N_SEL = 16
WINDOW = 512
N_EXPERT_GROUPS = 8
TOPK_GROUPS = 4
TOP_K = 8
ROUTED_SCALE = 2.5
MOE_BLOCK = 128
EPS = 1e-6
NEG = -1e30
BIG = 1e30
LOWEST = -3.0e38

LANES = 128
SUBLANES = 8
VMEM_LIMIT = 56 * 1024 * 1024
PAGES_PER_STEP = 8
SUPER_BLOCKS = 4
HIDDEN_CHUNK = 256


def _tile(n, pref, align):
    t = (min(pref, n) // align) * align
    while t >= align:
        if n % t == 0:
            return t
        t -= align
    return n


def _round_up(n, m):
    return -(-n // m) * m


def _params(sem, vmem=VMEM_LIMIT, **kw):
    return pltpu.CompilerParams(dimension_semantics=sem, vmem_limit_bytes=vmem, **kw)


def _dot(a, b):
    return jnp.dot(a.astype(BF16), b.astype(BF16), preferred_element_type=F32)


def _dot_nt(a, b):
    return lax.dot_general(a.astype(BF16), b.astype(BF16), (((1,), (1,)), ((), ())),
                           preferred_element_type=F32)


def _dot_f32(a, b):
    return jnp.dot(a, b, preferred_element_type=F32, precision=HIGHEST)


def _silu(x):
    return x * (1.0 / (1.0 + jnp.exp(-x)))


def _sigmoid(x):
    return 1.0 / (1.0 + jnp.exp(-x))


def _matmul(a, w, *, tm, tn, tk, out_dtype, name, m_rows=None, a_row_off=0,
            extras=(), extra_specs=(), prologue=None, epilogue=None):
    m = m_rows if m_rows is not None else a.shape[0]
    k_dim, n = w.shape
    assert m % tm == 0 and n % tn == 0 and k_dim % tk == 0 and a_row_off % tm == 0
    nk = k_dim // tk
    ne = len(extras)
    roff = a_row_off // tm

    def body(a_ref, w_ref, *refs):
        ex, o_ref, acc = refs[:ne], refs[ne], refs[ne + 1]
        k = pl.program_id(2)
        av = a_ref[...]
        if prologue is not None:
            av = prologue(av)
        part = _dot(av, w_ref[...])

        @pl.when(k == 0)
        def _():
            acc[...] = part

        @pl.when(k > 0)
        def _():
            acc[...] += part

        @pl.when(k == nk - 1)
        def _():
            r = acc[...]
            if epilogue is not None:
                r = epilogue(r, *[e[...] for e in ex])
            o_ref[...] = r.astype(out_dtype)

    return pl.pallas_call(
        body,
        out_shape=jax.ShapeDtypeStruct((m, n), out_dtype),
        grid=(m // tm, n // tn, nk),
        in_specs=[pl.BlockSpec((tm, tk), lambda i, j, k: (i + roff, k)),
                  pl.BlockSpec((tk, tn), lambda i, j, k: (k, j))] + list(extra_specs),
        out_specs=pl.BlockSpec((tm, tn), lambda i, j, k: (i, j)),
        scratch_shapes=[pltpu.VMEM((tm, tn), F32)],
        compiler_params=_params(("parallel", "parallel", "arbitrary")),
        name=name,
    )(a, w, *extras)


def _norm_mod(x, gain, scale, shift, *, tt, name, router_t=None):
    m, d = x.shape
    per_row = scale.shape[0] != 1
    mod_spec = (pl.BlockSpec((tt, d), lambda i: (i, 0)) if per_row
                else pl.BlockSpec((1, d), lambda i: (0, 0)))
    routed = router_t is not None

    def body(x_ref, g_ref, sc_ref, sh_ref, *refs):
        xv = x_ref[...]
        y = xv * lax.rsqrt(jnp.mean(xv * xv, axis=-1, keepdims=True) + EPS) * g_ref[...]
        h = y * (1.0 + sc_ref[...]) + sh_ref[...]
        if not routed:
            refs[0][...] = h.astype(BF16)
            return
        r_ref, p_ref, l_ref = refs
        hb = lax.bitcast_convert_type(h.astype(BF16).astype(F32), jnp.uint32)
        half = d // 2
        p_ref[...] = (hb[:, half:] & jnp.uint32(0xFFFF0000)) | (hb[:, :half] >> 16)
        l_ref[...] = lax.dot_general(r_ref[...], h, (((1,), (1,)), ((), ())),
                                     preferred_element_type=F32, precision=HIGHEST)

    in_specs = [pl.BlockSpec((tt, d), lambda i: (i, 0)), pl.BlockSpec((1, d), lambda i: (0, 0)),
                mod_spec, mod_spec]
    args = [x, gain.reshape(1, d), scale, shift]
    if routed:
        e = router_t.shape[0]
        in_specs.append(pl.BlockSpec((e, d), lambda i: (0, 0)))
        args.append(router_t)
        out_shape = (jax.ShapeDtypeStruct((m, d // 2), jnp.uint32), jax.ShapeDtypeStruct((e, m), F32))
        out_specs = (pl.BlockSpec((tt, d // 2), lambda i: (i, 0)), pl.BlockSpec((e, tt), lambda i: (0, i)))
    else:
        out_shape = jax.ShapeDtypeStruct((m, d), BF16)
        out_specs = pl.BlockSpec((tt, d), lambda i: (i, 0))
    return pl.pallas_call(body, out_shape=out_shape, grid=(m // tt,), in_specs=in_specs,
                          out_specs=out_specs, compiler_params=_params(("parallel",)), name=name)(*args)


def _head_prep(proj, col0, n_heads, gain, rope_tabs, *, tr, name):
    rows = proj.shape[0]
    width = n_heads * HEAD_DIM
    assert col0 % width == 0
    do_norm, do_rope = gain is not None, rope_tabs is not None
    half = ROPE_DIM // 2

    def body(*refs):
        x_ref = refs[0]
        pos = 1
        if do_norm:
            g_ref = refs[pos]
            pos += 1
        if do_rope:
            c_ref, sa_ref, sb_ref = refs[pos:pos + 3]
            pos += 3
        o_ref = refs[pos]
        for h in range(n_heads):
            x = x_ref[:, h * HEAD_DIM:(h + 1) * HEAD_DIM]
            if do_norm:
                x = x * lax.rsqrt(jnp.mean(x * x, axis=-1, keepdims=True) + EPS) * g_ref[...]
            if do_rope:
                x = (x * c_ref[...] + pltpu.roll(x, HEAD_DIM - half, 1) * sa_ref[...]
                     + pltpu.roll(x, half, 1) * sb_ref[...])
            o_ref[:, h * HEAD_DIM:(h + 1) * HEAD_DIM] = x

    in_specs = [pl.BlockSpec((tr, width), lambda i: (i, col0 // width))]
    args = [proj]
    if do_norm:
        in_specs.append(pl.BlockSpec((1, HEAD_DIM), lambda i: (0, 0)))
        args.append(gain.reshape(1, HEAD_DIM))
    if do_rope:
        in_specs += [pl.BlockSpec((tr, HEAD_DIM), lambda i: (i, 0))] * 3
        args += list(rope_tabs)
    return pl.pallas_call(body, out_shape=jax.ShapeDtypeStruct((rows, width), F32),
                          grid=(rows // tr,), in_specs=in_specs,
                          out_specs=pl.BlockSpec((tr, width), lambda i: (i, 0)),
                          compiler_params=_params(("parallel",)), name=name)(*args)


def _gates(proj, col0, b_forget, *, tr, name):
    rows = proj.shape[0]
    bias = jnp.zeros((1, LANES), F32).at[0, :FOX_HEADS].set(b_forget.astype(F32))

    def body(x_ref, b_ref, o_ref):
        x = x_ref[...] + b_ref[...]
        lane = lax.broadcasted_iota(jnp.int32, x.shape, 1)
        logsig = jnp.minimum(x, 0.0) - jnp.log(1.0 + jnp.exp(-jnp.abs(x)))
        o_ref[...] = jnp.where(lane < FOX_HEADS, logsig, _sigmoid(x))

    return pl.pallas_call(body, out_shape=jax.ShapeDtypeStruct((rows, LANES), F32),
                          grid=(rows // tr,),
                          in_specs=[pl.BlockSpec((tr, LANES), lambda i: (i, col0 // LANES)),
                                    pl.BlockSpec((1, LANES), lambda i: (0, 0))],
                          out_specs=pl.BlockSpec((tr, LANES), lambda i: (i, 0)),
                          compiler_params=_params(("parallel",)), name=name)(proj, bias)


def _page_specs(n_pages, block_tail, tail_index, page_of_step):
    def make(j):
        def index(*a):
            *gi, pt = a
            b = gi[0]
            return (pt[b, page_of_step(*gi) + j],) + tuple(tail_index(*gi))
        return pl.BlockSpec((1,) + tuple(block_tail), index)
    return [make(j) for j in range(n_pages)]


def _cumsum_logf(pages, page_table, extra, *, name):
    b_sz, n_pg = page_table.shape
    p = PAGES_PER_STEP
    assert n_pg % p == 0 and LANES % FOX_HEADS == 0
    rp = pages.shape[1]
    rows = p * rp
    nst = n_pg // p
    has_x = extra is not None
    steps = nst + (1 if has_x else 0)

    def body(pt_ref, *refs):
        prefs = refs[:p]
        x_ref = refs[p] if has_x else None
        o_ref, carry = refs[p + has_x], refs[p + has_x + 1]
        c = pl.program_id(1)

        @pl.when(c == 0)
        def _():
            carry[...] = jnp.zeros_like(carry)

        def run(x):
            ri = lax.broadcasted_iota(jnp.int32, (LANES, LANES), 0)
            ci = lax.broadcasted_iota(jnp.int32, (LANES, LANES), 1)
            same = (ri % FOX_HEADS) == (ci % FOX_HEADS)
            within = (same & (ri // FOX_HEADS <= ci // FOX_HEADS)).astype(F32)
            total = _dot_f32(x, same.astype(F32))
            rr = lax.broadcasted_iota(jnp.int32, (rows, rows), 0)
            rc = lax.broadcasted_iota(jnp.int32, (rows, rows), 1)
            before = _dot_f32((rc < rr).astype(F32), total)
            o_ref[0] = _dot_f32(x, within) + before + carry[...]
            carry[...] += jnp.sum(total, axis=0, keepdims=True)

        if has_x:
            @pl.when(c < nst)
            def _():
                run(jnp.concatenate([r[0] for r in prefs], axis=0))

            @pl.when(c == nst)
            def _():
                run(x_ref[0])
        else:
            run(jnp.concatenate([r[0] for r in prefs], axis=0))

    in_specs = _page_specs(p, (rp, LANES), lambda b, c: (0, 0),
                           lambda b, c: jnp.minimum(c, nst - 1) * p)
    args = [pages] * p
    if has_x:
        in_specs.append(pl.BlockSpec((1, rows, LANES), lambda b, c, pt: (b, 0, 0)))
        args.append(extra)
    out = pl.pallas_call(
        body,
        out_shape=jax.ShapeDtypeStruct((b_sz, steps * rows, LANES), F32),
        grid_spec=pltpu.PrefetchScalarGridSpec(
            num_scalar_prefetch=1, grid=(b_sz, steps), in_specs=in_specs,
            out_specs=pl.BlockSpec((1, rows, LANES), lambda b, c, pt: (b, c, 0)),
            scratch_shapes=[pltpu.VMEM((1, LANES), F32)]),
        compiler_params=_params(("parallel", "arbitrary")), name=name,
    )(page_table, *args)
    return out.reshape(b_sz, steps * rows * LANES // FOX_HEADS, FOX_HEADS)


def _compress_first(pages, page_table, w1ab, n_groups, *, name):
    b_sz, n_pg = page_table.shape
    p = PAGES_PER_STEP
    assert n_pg % p == 0
    cpp = pages.shape[1]
    roww = n_groups * HEAD_DIM
    n2 = w1ab.shape[1]
    rows = p * cpp

    def body(pt_ref, *refs):
        prefs, w_ref, o_ref = refs[:p], refs[p], refs[p + 1]
        x = jnp.concatenate([r[0] for r in prefs], axis=0)
        lhs = jnp.concatenate(
            [jnp.concatenate([x[:, l * roww + g * HEAD_DIM: l * roww + (g + 1) * HEAD_DIM]
                              for l in range(CMP_STRIDE)], axis=1) for g in range(n_groups)], axis=0)
        res = _dot(lhs, w_ref[...])
        for g in range(n_groups):
            o_ref[0, :, g * n2:(g + 1) * n2] = res[g * rows:(g + 1) * rows]

    in_specs = _page_specs(p, (cpp, pages.shape[2]), lambda b, c: (0, 0), lambda b, c: c * p)
    in_specs.append(pl.BlockSpec(w1ab.shape, lambda b, c, pt: (0, 0)))
    return pl.pallas_call(
        body,
        out_shape=jax.ShapeDtypeStruct((b_sz, n_pg * cpp, n_groups * n2), F32),
        grid_spec=pltpu.PrefetchScalarGridSpec(
            num_scalar_prefetch=1, grid=(b_sz, n_pg // p), in_specs=in_specs,
            out_specs=pl.BlockSpec((1, rows, n_groups * n2), lambda b, c, pt: (b, c, 0))),
        compiler_params=_params(("parallel", "parallel")), name=name,
    )(page_table, *([pages] * p), w1ab)


def _compress_second(papb, pb_next, cpe, w2, gain, n_groups, *, name):
    b_sz, nch, _ = papb.shape
    hid = w2.shape[0]
    do_norm = gain is not None
    g_arr = (gain if do_norm else jnp.ones((HEAD_DIM,), F32)).reshape(1, HEAD_DIM)

    def body(x_ref, n_ref, c_ref, w_ref, g_ref, o_ref):
        row = lax.broadcasted_iota(jnp.int32, (nch, hid), 0)
        for g in range(n_groups):
            pa = x_ref[0, :, g * 2 * hid: g * 2 * hid + hid]
            pb = x_ref[0, :, g * 2 * hid + hid:(g + 1) * 2 * hid]
            nxt = jnp.where(row == nch - 1, n_ref[0, :, g * hid:(g + 1) * hid],
                            pltpu.roll(pb, nch - 1, 0))
            pre = pa + nxt + c_ref[...]
            act = 0.5 * pre * (1.0 + jnp.tanh(0.7978845608028654 * (pre + 0.044715 * pre * pre * pre)))
            y = _dot(act, w_ref[...])
            if do_norm:
                y = y * lax.rsqrt(jnp.mean(y * y, axis=-1, keepdims=True) + EPS) * g_ref[...]
            o_ref[0, :, g * HEAD_DIM:(g + 1) * HEAD_DIM] = y

    return pl.pallas_call(
        body, out_shape=jax.ShapeDtypeStruct((b_sz, nch, n_groups * HEAD_DIM), F32), grid=(b_sz,),
        in_specs=[pl.BlockSpec((1, nch, papb.shape[2]), lambda b: (b, 0, 0)),
                  pl.BlockSpec((1, 1, n_groups * hid), lambda b: (b, 0, 0)),
                  pl.BlockSpec((1, hid), lambda b: (0, 0)),
                  pl.BlockSpec(w2.shape, lambda b: (0, 0)),
                  pl.BlockSpec((1, HEAD_DIM), lambda b: (0, 0))],
        out_specs=pl.BlockSpec((1, nch, n_groups * HEAD_DIM), lambda b: (b, 0, 0)),
        compiler_params=_params(("parallel",)), name=name,
    )(papb, pb_next, cpe, w2, g_arr)


def _cmp_select(q, ck, cv, overlap, *, tq, q0, nc_valid, n_blocks, name):
    b_sz, t_q, hw = q.shape
    nch = ck.shape[1]
    n_groups = ck.shape[2] // HEAD_DIM
    rep = hw // HEAD_DIM // n_groups
    nsp = overlap.shape[1]
    n_keep = min(N_SEL, n_blocks)
    scale = HEAD_DIM ** -0.5

    def body(q_ref, k_ref, v_ref, ov_ref, o_ref, m_ref):
        qi = pl.program_id(2)
        qv = q_ref[0]
        q4 = jnp.concatenate([qv[:, r * HEAD_DIM:(r + 1) * HEAD_DIM] for r in range(rep)], axis=0)
        s = _dot_nt(q4, k_ref[0]) * scale
        qpos = q0 + qi * tq + lax.broadcasted_iota(jnp.int32, (tq, 1), 0)
        cidx = lax.broadcasted_iota(jnp.int32, (1, nch), 1)
        cmask = (cidx * CMP_STRIDE + (CMP_LEN - 1) <= qpos) & (cidx < nc_valid)
        s3 = jnp.where(cmask[None], s.reshape(rep, tq, nch), NEG)
        mx = jnp.max(s3, axis=-1, keepdims=True)
        e = jnp.where(cmask[None], jnp.exp(s3 - mx), 0.0)
        den = jnp.sum(e, axis=-1, keepdims=True)
        p3 = e / jnp.where(den > 0.0, den, 1.0)
        o = _dot(p3.reshape(rep * tq, nch), v_ref[0])
        for r in range(rep):
            o_ref[0, :, r * HEAD_DIM:(r + 1) * HEAD_DIM] = o[r * tq:(r + 1) * tq].astype(BF16)
        imp = _dot_f32(jnp.sum(p3, axis=0), ov_ref[...])
        blk = lax.broadcasted_iota(jnp.int32, (1, nsp), 1)
        cur = qpos // SEL_LEN
        valid = (blk * SEL_LEN <= qpos) & (blk < n_blocks)
        forced = (blk == 0) | (blk == cur) | (blk == cur - 1)
        score = jnp.where(valid & forced, BIG, jnp.where(valid, imp, NEG))
        keep = jnp.zeros((tq, nsp), jnp.bool_)
        for _ in range(n_keep):
            best = jnp.max(score, axis=-1, keepdims=True)
            first = jnp.min(jnp.where(score == best, blk, nsp), axis=-1, keepdims=True)
            hit = blk == first
            keep = keep | hit
            score = jnp.where(hit, LOWEST, score)
        m_ref[0] = jnp.where(keep & valid, 1.0, 0.0).astype(BF16)

    gw = rep * HEAD_DIM
    return pl.pallas_call(
        body,
        out_shape=(jax.ShapeDtypeStruct((b_sz, t_q, hw), BF16),
                   jax.ShapeDtypeStruct((b_sz, t_q, n_groups * nsp), BF16)),
        grid=(b_sz, n_groups, t_q // tq),
        in_specs=[pl.BlockSpec((1, tq, gw), lambda b, g, i: (b, i, g)),
                  pl.BlockSpec((1, nch, HEAD_DIM), lambda b, g, i: (b, 0, g)),
                  pl.BlockSpec((1, nch, HEAD_DIM), lambda b, g, i: (b, 0, g)),
                  pl.BlockSpec(overlap.shape, lambda b, g, i: (0, 0))],
        out_specs=(pl.BlockSpec((1, tq, gw), lambda b, g, i: (b, i, g)),
                   pl.BlockSpec((1, tq, nsp), lambda b, g, i: (b, i, g))),
        compiler_params=_params(("parallel", "parallel", "parallel")), name=name,
    )(q, ck, cv, overlap)


def _attention(q, ksrc, vsrc, page_table, *, tq, n_pages, q0, pos_base, name, fuse_groups,
               extra=None, extra_pos=0, decay=None, block_mask=None, window=None,
               band_steps=None):
    b_sz, t_q, hw = q.shape
    pg = ksrc.shape[1]
    n_groups = ksrc.shape[2] // HEAD_DIM
    rep = hw // HEAD_DIM // n_groups
    n_pg = page_table.shape[1]
    p = n_pages
    assert n_pg % p == 0 and t_q % tq == 0
    nst = n_pg // p
    tk = p * pg
    banded = band_steps is not None
    if banded:
        assert p == 1 and tq == pg and window == pg and extra is None
    has_x = extra is not None
    main_steps = band_steps if banded else nst
    steps = main_steps + (1 if has_x else 0)
    scale = HEAD_DIM ** -0.5
    has_decay, has_bm = decay is not None, block_mask is not None
    nsp = block_mask.shape[2] // n_groups if has_bm else 0
    groups = range(n_groups) if fuse_groups else (None,)
    ng_blk = n_groups if fuse_groups else 1
    kw_blk = ng_blk * HEAD_DIM
    qw_blk = ng_blk * rep * HEAD_DIM

    def norm(gi):
        if fuse_groups:
            b, i, c = gi
            return b, 0, i, c
        return gi

    def last_step(i):
        return jnp.clip((q0 + (i + 1) * tq - 1 - pos_base) // tk, 0, nst - 1)

    def first_page(*gi):
        b, g, i, c = norm(gi)
        if banded:
            return jnp.maximum(i - (band_steps - 1) + c, 0)
        return jnp.minimum(c, last_step(i)) * p

    def body(pt_ref, *refs):
        pos = 0
        q_ref = refs[pos]; pos += 1
        k_refs = refs[pos:pos + p]; pos += p
        v_refs = refs[pos:pos + p]; pos += p
        if has_x:
            kx_ref, vx_ref = refs[pos:pos + 2]; pos += 2
        if has_decay:
            fq_ref, ft_ref = refs[pos:pos + 2]; pos += 2
        if has_bm:
            bm_ref = refs[pos]; pos += 1
        o_ref, m_scr, l_scr, a_scr = refs[pos:pos + 4]
        i = pl.program_id(1 if fuse_groups else 2)
        c = pl.program_id(2 if fuse_groups else 3)

        @pl.when(c == 0)
        def _():
            m_scr[...] = jnp.full(m_scr.shape, NEG, F32)
            l_scr[...] = jnp.zeros(l_scr.shape, F32)
            a_scr[...] = jnp.zeros(a_scr.shape, F32)

        qpos = q0 + i * tq + lax.broadcasted_iota(jnp.int32, (tq, 1), 0)

        def tile(gi, kt, vt, kpos, ft_t):
            go = 0 if gi is None else gi
            n = kt.shape[0]
            qv = q_ref[0]
            q4 = jnp.concatenate([qv[:, (go * rep + r) * HEAD_DIM:(go * rep + r + 1) * HEAD_DIM]
                                  for r in range(rep)], axis=0)
            s3 = (_dot_nt(q4, kt) * scale).reshape(rep, tq, n)
            if has_decay:
                fq_t = fq_ref[0, go]
                s3 = s3 + jnp.stack([fq_t[:, r:r + 1] - ft_t[r:r + 1, :] for r in range(rep)], axis=0)
            vis = kpos <= qpos
            if window is not None:
                vis = vis & (qpos - kpos < window) & (kpos >= pos_base)
            if has_bm:
                sel = bm_ref[0][:, go * nsp:(go + 1) * nsp]
                blk = lax.broadcasted_iota(jnp.int32, (nsp, n), 0)
                expand = (blk == kpos // SEL_LEN).astype(BF16)
                vis = vis & (jnp.dot(sel, expand, preferred_element_type=F32) > 0.5)
            s3 = jnp.where(vis[None], s3, NEG)
            m_old = m_scr[go]
            m_new = jnp.maximum(m_old, jnp.max(s3, axis=-1, keepdims=True))
            alpha = jnp.exp(m_old - m_new)
            pr = jnp.exp(s3 - m_new)
            l_scr[go] = alpha * l_scr[go] + jnp.sum(pr, axis=-1, keepdims=True)
            pv = _dot(pr.reshape(rep * tq, n), vt).reshape(rep, tq, HEAD_DIM)
            a_scr[go] = alpha * a_scr[go] + pv
            m_scr[go] = m_new

        def main():
            if banded:
                start = (i - (band_steps - 1) + c) * pg + pos_base
            else:
                start = c * tk + pos_base
            kpos = start + lax.broadcasted_iota(jnp.int32, (1, tk), 1)
            for gi in groups:
                go = 0 if gi is None else gi
                lo, hi = go * HEAD_DIM, (go + 1) * HEAD_DIM
                kt = jnp.concatenate([r[0][:, lo:hi] for r in k_refs], axis=0)
                vt = jnp.concatenate([r[0][:, lo:hi] for r in v_refs], axis=0)
                ft_t = ft_ref[0, go] if has_decay else None
                tile(gi, kt, vt, kpos, ft_t)

        if banded:
            main()
        else:
            needed = c <= last_step(i)
            if has_x:
                needed = needed & (c < nst)
            pl.when(needed)(main)

        if has_x:
            @pl.when(c == nst)
            def _():
                nx = kx_ref.shape[1]
                kpos = extra_pos + lax.broadcasted_iota(jnp.int32, (1, nx), 1)
                for gi in groups:
                    go = 0 if gi is None else gi
                    lo, hi = go * HEAD_DIM, (go + 1) * HEAD_DIM
                    ft_t = ft_ref[0, go][:, :nx] if has_decay else None
                    tile(gi, kx_ref[0][:, lo:hi], vx_ref[0][:, lo:hi], kpos, ft_t)

        @pl.when(c == steps - 1)
        def _():
            for gi in groups:
                go = 0 if gi is None else gi
                out = a_scr[go] / l_scr[go]
                for r in range(rep):
                    col = (go * rep + r) * HEAD_DIM
                    o_ref[0, :, col:col + HEAD_DIM] = out[r].astype(BF16)

    def gsel(gi):
        b, g, i, c = norm(gi)
        return b, g, i, c

    in_specs = [pl.BlockSpec((1, tq, qw_blk), lambda *a: (gsel(a[:-1])[0], gsel(a[:-1])[2], gsel(a[:-1])[1]))]
    args = [q]
    kv_tail = lambda *gi: (0, norm(gi)[1])
    in_specs += _page_specs(p, (pg, kw_blk), kv_tail, first_page)
    in_specs += _page_specs(p, (pg, kw_blk), kv_tail, first_page)
    args += [ksrc] * p + [vsrc] * p
    if has_x:
        pgx = extra[0].shape[1]
        xs = pl.BlockSpec((1, pgx, kw_blk), lambda *a: (norm(a[:-1])[0], 0, norm(a[:-1])[1]))
        in_specs += [xs, xs]
        args += list(extra)
    if has_decay:
        fq, ft = decay
        in_specs.append(pl.BlockSpec((1, ng_blk, tq, rep),
                                     lambda *a: (norm(a[:-1])[0], norm(a[:-1])[1], norm(a[:-1])[2], 0)))

        def ft_index(*a):
            b, g, i, c = norm(a[:-1])
            step = jnp.where(c >= nst, nst, jnp.minimum(c, last_step(i))) if has_x else jnp.minimum(c, last_step(i))
            return (b, g, 0, step)
        in_specs.append(pl.BlockSpec((1, ng_blk, rep, tk), ft_index))
        args += [fq, ft]
    if has_bm:
        in_specs.append(pl.BlockSpec((1, tq, ng_blk * nsp),
                                     lambda *a: (norm(a[:-1])[0], norm(a[:-1])[2], norm(a[:-1])[1])))
        args.append(block_mask)
    grid = (b_sz, t_q // tq, steps) if fuse_groups else (b_sz, n_groups, t_q // tq, steps)
    sem = ("parallel",) * (len(grid) - 1) + ("arbitrary",)
    return pl.pallas_call(
        body,
        out_shape=jax.ShapeDtypeStruct((b_sz, t_q, hw), BF16),
        grid_spec=pltpu.PrefetchScalarGridSpec(
            num_scalar_prefetch=1, grid=grid, in_specs=in_specs,
            out_specs=pl.BlockSpec((1, tq, qw_blk),
                                   lambda *a: (norm(a[:-1])[0], norm(a[:-1])[2], norm(a[:-1])[1])),
            scratch_shapes=[pltpu.VMEM((ng_blk, rep, tq, 1), F32), pltpu.VMEM((ng_blk, rep, tq, 1), F32),
                            pltpu.VMEM((ng_blk, rep, tq, HEAD_DIM), F32)]),
        compiler_params=_params(sem), name=name,
    )(page_table, *args)


def _nsa_combine(o_cmp, o_sel, o_win, gates, *, tr, name):
    rows, hw = o_cmp.shape

    def body(c_ref, s_ref, w_ref, g_ref, o_ref):
        g = g_ref[...]
        for h in range(NSA_HEADS):
            sl = slice(h * HEAD_DIM, (h + 1) * HEAD_DIM)
            lane = FOX_HEADS + h
            acc = (g[:, lane:lane + 1] * c_ref[:, sl].astype(F32)
                   + g[:, lane + NSA_HEADS:lane + NSA_HEADS + 1] * s_ref[:, sl].astype(F32)
                   + g[:, lane + 2 * NSA_HEADS:lane + 2 * NSA_HEADS + 1] * w_ref[:, sl].astype(F32))
            o_ref[:, sl] = acc.astype(BF16)

    spec = pl.BlockSpec((tr, hw), lambda i: (i, 0))
    return pl.pallas_call(body, out_shape=jax.ShapeDtypeStruct((rows, hw), BF16), grid=(rows // tr,),
                          in_specs=[spec, spec, spec, pl.BlockSpec((tr, LANES), lambda i: (i, 0))],
                          out_specs=spec, compiler_params=_params(("parallel",)), name=name,
                          )(o_cmp, o_sel, o_win, gates)


def _merge(o_fox, o_nsa, w_branch, proj, col_mg, *, tm, tn, tk, name):
    m, kd = o_fox.shape
    d = w_branch.shape[2]
    nk = kd // tk
    assert col_mg % tn == 0 and d % tn == 0

    def body(a0_ref, a1_ref, w_ref, g0_ref, g1_ref, o_ref, acc0, acc1):
        k = pl.program_id(2)
        p0 = _dot(a0_ref[...], w_ref[0])
        p1 = _dot(a1_ref[...], w_ref[1])

        @pl.when(k == 0)
        def _():
            acc0[...] = p0
            acc1[...] = p1

        @pl.when(k > 0)
        def _():
            acc0[...] += p0
            acc1[...] += p1

        @pl.when(k == nk - 1)
        def _():
            o_ref[...] = (_sigmoid(g0_ref[...]) * acc0[...] + _sigmoid(g1_ref[...]) * acc1[...]).astype(BF16)

    a_spec = pl.BlockSpec((tm, tk), lambda i, j, k: (i, k))
    return pl.pallas_call(
        body, out_shape=jax.ShapeDtypeStruct((m, d), BF16), grid=(m // tm, d // tn, nk),
        in_specs=[a_spec, a_spec, pl.BlockSpec((2, tk, tn), lambda i, j, k: (0, k, j)),
                  pl.BlockSpec((tm, tn), lambda i, j, k: (i, col_mg // tn + j)),
                  pl.BlockSpec((tm, tn), lambda i, j, k: (i, (col_mg + d) // tn + j))],
        out_specs=pl.BlockSpec((tm, tn), lambda i, j, k: (i, j)),
        scratch_shapes=[pltpu.VMEM((tm, tn), F32), pltpu.VMEM((tm, tn), F32)],
        compiler_params=_params(("parallel", "parallel", "arbitrary")), name=name,
    )(o_fox, o_nsa, w_branch, proj, proj)


def _route(logits_t, router_bias, *, tn, name):
    n_exp, n_tok = logits_t.shape
    per = n_exp // N_EXPERT_GROUPS
    assert per == SUBLANES

    def body(l_ref, b_ref, e_ref, w_ref):
        scores = _sigmoid(l_ref[...])
        biased = scores + b_ref[...]
        b3 = biased.reshape(N_EXPERT_GROUPS, per, tn)
        sub = lax.broadcasted_iota(jnp.int32, b3.shape, 1)
        m1 = jnp.max(b3, axis=1, keepdims=True)
        i1 = jnp.min(jnp.where(b3 == m1, sub, per), axis=1, keepdims=True)
        m2 = jnp.max(jnp.where(sub == i1, LOWEST, b3), axis=1, keepdims=True)
        gscore = (m1 + m2).reshape(N_EXPERT_GROUPS, tn)
        gid = lax.broadcasted_iota(jnp.int32, gscore.shape, 0)
        gkeep = jnp.zeros(gscore.shape, jnp.bool_)
        for _ in range(TOPK_GROUPS):
            best = jnp.max(gscore, axis=0, keepdims=True)
            first = jnp.min(jnp.where(gscore == best, gid, N_EXPERT_GROUPS), axis=0, keepdims=True)
            hit = gid == first
            gkeep = gkeep | hit
            gscore = jnp.where(hit, LOWEST, gscore)
        emask = jnp.broadcast_to(gkeep[:, None, :], b3.shape).reshape(n_exp, tn)
        masked = jnp.where(emask, biased, NEG)
        eid = lax.broadcasted_iota(jnp.int32, masked.shape, 0)
        ids, wts = [], []
        for _ in range(TOP_K):
            best = jnp.max(masked, axis=0, keepdims=True)
            first = jnp.min(jnp.where(masked == best, eid, n_exp), axis=0, keepdims=True)
            hit = eid == first
            ids.append(first)
            wts.append(jnp.sum(jnp.where(hit, scores, 0.0), axis=0, keepdims=True))
            masked = jnp.where(hit, LOWEST, masked)
        w = jnp.concatenate(wts, axis=0)
        e_ref[...] = jnp.concatenate(ids, axis=0)
        w_ref[...] = w / jnp.sum(w, axis=0, keepdims=True) * ROUTED_SCALE

    return pl.pallas_call(
        body,
        out_shape=(jax.ShapeDtypeStruct((TOP_K, n_tok), jnp.int32), jax.ShapeDtypeStruct((TOP_K, n_tok), F32)),
        grid=(n_tok // tn,),
        in_specs=[pl.BlockSpec((n_exp, tn), lambda i: (0, i)), pl.BlockSpec((n_exp, 1), lambda i: (0, 0))],
        out_specs=(pl.BlockSpec((TOP_K, tn), lambda i: (0, i)), pl.BlockSpec((TOP_K, tn), lambda i: (0, i))),
        compiler_params=_params(("parallel",)), name=name,
    )(logits_t, router_bias.astype(F32).reshape(n_exp, 1))


def _dispatch_tables(eidx_t, n_exp):
    n_tok = eidx_t.shape[1]
    a_cnt = n_tok * TOP_K
    flat_e = eidx_t.T.reshape(a_cnt)
    order = jnp.argsort(flat_e).astype(jnp.int32)
    counts = jnp.sum(flat_e[None, :] == jnp.arange(n_exp, dtype=jnp.int32)[:, None], axis=1).astype(jnp.int32)
    blocks = (counts + MOE_BLOCK - 1) // MOE_BLOCK
    start = jnp.cumsum(counts) - counts
    sb_cnt = (blocks + SUPER_BLOCKS - 1) // SUPER_BLOCKS
    sb_end = jnp.cumsum(sb_cnt)
    n_super = -(-(-(-a_cnt // MOE_BLOCK) + n_exp) // SUPER_BLOCKS) + n_exp
    sid = jnp.arange(n_super, dtype=jnp.int32)
    used = sid < sb_end[-1]
    e_of = jnp.minimum(jnp.searchsorted(sb_end, sid, side='right'), n_exp - 1).astype(jnp.int32)
    e_of = jnp.where(used, e_of, e_of[jnp.maximum(sb_end[-1] - 1, 0)])
    local = sid - (sb_end - sb_cnt)[e_of]
    nb = jnp.where(used, jnp.clip(blocks[e_of] - local * SUPER_BLOCKS, 0, SUPER_BLOCKS), 0).astype(jnp.int32)
    rows = SUPER_BLOCKS * MOE_BLOCK
    r = jnp.arange(rows, dtype=jnp.int32)[None, :]
    within = local[:, None] * rows + r
    real = used[:, None] & (within < counts[e_of][:, None])
    src = jnp.clip(start[e_of][:, None] + within, 0, a_cnt - 1)
    aid = order[src]
    tok = jnp.where(real, aid // TOP_K, 0).astype(jnp.int32)
    dst = jnp.where(real, aid, a_cnt + r).astype(jnp.int32)
    return e_of, nb, tok.reshape(-1), dst.reshape(-1), n_super


def _experts(packed, e_of, nb, tok, dst, w_gate, w_up, w_down, n_super, n_assign, *, name):
    n_exp, d, hid = w_gate.shape
    half = d // 2
    rows = SUPER_BLOCKS * MOE_BLOCK
    hc = min(HIDDEN_CHUNK, hid)
    n_hc = hid // hc

    def body(e_ref, nb_ref, tok_hbm, dst_hbm, x_hbm, wg_ref, wu_ref, wd_ref, y_hbm,
             tok_s, dst_s, x_scr, y_scr, wg_b, wu_b, wd_b, sem_i, sem_g, sem_s):
        s = pl.program_id(0)
        h = pl.program_id(1)
        n_blk = nb_ref[s]
        n_rows = n_blk * MOE_BLOCK

        def gather_copy(i):
            return pltpu.make_async_copy(x_hbm.at[pl.ds(tok_s[i], 1)], x_scr.at[pl.ds(i, 1)], sem_g)

        def scatter_copy(i):
            return pltpu.make_async_copy(y_scr.at[pl.ds(i, 1)], y_hbm.at[pl.ds(dst_s[i], 1)], sem_s)

        @pl.when((s == 0) & (h == 0))
        def _():
            y_scr[...] = jnp.zeros(y_scr.shape, F32)
            cp = pltpu.make_async_copy(y_scr, y_hbm.at[pl.ds(n_assign, rows)], sem_s)
            cp.start()
            cp.wait()

        @pl.when((n_blk > 0) & (h == 0))
        def _():
            base = pl.multiple_of(s * rows, rows)
            c1 = pltpu.make_async_copy(tok_hbm.at[pl.ds(base, rows)], tok_s, sem_i.at[0])
            c2 = pltpu.make_async_copy(dst_hbm.at[pl.ds(base, rows)], dst_s, sem_i.at[1])
            c1.start()
            c2.start()
            c1.wait()
            c2.wait()

            def issue(i, carry):
                gather_copy(i).start()
                return carry
            lax.fori_loop(0, n_rows, issue, 0)

            def drain(i, carry):
                gather_copy(i).wait()
                return carry
            lax.fori_loop(0, n_rows, drain, 0)

        @pl.when(n_blk > 0)
        def _():
            wg_b[...] = wg_ref[0].astype(BF16)
            wu_b[...] = wu_ref[0].astype(BF16)
            wd_b[...] = wd_ref[0].astype(BF16)

            def block(j, carry):
                r0 = pl.multiple_of(j * MOE_BLOCK, MOE_BLOCK)
                u = x_scr[pl.ds(r0, MOE_BLOCK), :]
                x_lo = lax.bitcast_convert_type(u << 16, F32).astype(BF16)
                x_hi = lax.bitcast_convert_type(u & jnp.uint32(0xFFFF0000), F32).astype(BF16)
                gate = (jnp.dot(x_lo, wg_b[:half], preferred_element_type=F32)
                        + jnp.dot(x_hi, wg_b[half:], preferred_element_type=F32))
                up = (jnp.dot(x_lo, wu_b[:half], preferred_element_type=F32)
                      + jnp.dot(x_hi, wu_b[half:], preferred_element_type=F32))
                part = jnp.dot((_silu(gate) * up).astype(BF16), wd_b[...], preferred_element_type=F32)

                @pl.when(h == 0)
                def _():
                    y_scr[pl.ds(r0, MOE_BLOCK), :] = part

                @pl.when(h > 0)
                def _():
                    y_scr[pl.ds(r0, MOE_BLOCK), :] += part
                return carry
            lax.fori_loop(0, n_blk, block, 0)

        @pl.when((n_blk > 0) & (h == n_hc - 1))
        def _():
            def issue(i, carry):
                scatter_copy(i).start()
                return carry
            lax.fori_loop(0, n_rows, issue, 0)

            def drain(i, carry):
                scatter_copy(i).wait()
                return carry
            lax.fori_loop(0, n_rows, drain, 0)

    any_spec = pl.BlockSpec(memory_space=pltpu.MemorySpace.HBM)
    return pl.pallas_call(
        body,
        out_shape=jax.ShapeDtypeStruct((n_assign + rows, d), F32),
        grid_spec=pltpu.PrefetchScalarGridSpec(
            num_scalar_prefetch=2, grid=(n_super, n_hc),
            in_specs=[any_spec, any_spec, any_spec,
                      pl.BlockSpec((1, d, hc), lambda s, h, e, nb: (e[s], 0, h)),
                      pl.BlockSpec((1, d, hc), lambda s, h, e, nb: (e[s], 0, h)),
                      pl.BlockSpec((1, hc, d), lambda s, h, e, nb: (e[s], h, 0))],
            out_specs=any_spec,
            scratch_shapes=[pltpu.SMEM((rows,), jnp.int32), pltpu.SMEM((rows,), jnp.int32),
                            pltpu.VMEM((rows, half), jnp.uint32), pltpu.VMEM((rows, d), F32),
                            pltpu.VMEM((d, hc), BF16), pltpu.VMEM((d, hc), BF16), pltpu.VMEM((hc, d), BF16),
                            pltpu.SemaphoreType.DMA((2,)), pltpu.SemaphoreType.DMA, pltpu.SemaphoreType.DMA]),
        compiler_params=_params(("arbitrary", "arbitrary"), has_side_effects=True),
        name=name,
    )(e_of, nb, tok, dst, packed, w_gate, w_up, w_down)


def _shared_hidden(packed, w_gate, w_up, *, tm, tn, name):
    m, half = packed.shape
    hid = w_gate.shape[1]

    def body(x_ref, wg_ref, wu_ref, o_ref):
        u = x_ref[...]
        x_lo = lax.bitcast_convert_type(u << 16, F32).astype(BF16)
        x_hi = lax.bitcast_convert_type(u & jnp.uint32(0xFFFF0000), F32).astype(BF16)
        gate = _dot(x_lo, wg_ref[:half]) + _dot(x_hi, wg_ref[half:])
        up = _dot(x_lo, wu_ref[:half]) + _dot(x_hi, wu_ref[half:])
        o_ref[...] = (_silu(gate) * up).astype(BF16)

    w_spec = pl.BlockSpec((2 * half, tn), lambda i, j: (0, j))
    return pl.pallas_call(
        body, out_shape=jax.ShapeDtypeStruct((m, hid), BF16), grid=(m // tm, hid // tn),
        in_specs=[pl.BlockSpec((tm, half), lambda i, j: (i, 0)), w_spec, w_spec],
        out_specs=pl.BlockSpec((tm, tn), lambda i, j: (i, j)),
        compiler_params=_params(("parallel", "parallel")), name=name,
    )(packed, w_gate, w_up)


def _rope_tables(pos):
    half = ROPE_DIM // 2
    inv = ROPE_THETA ** (-2.0 * jnp.arange(half, dtype=F32) / ROPE_DIM)
    ang = pos.astype(F32)[:, None] * inv[None, :]
    cos, sin = jnp.cos(ang), jnp.sin(ang)
    n = pos.shape[0]
    rest = HEAD_DIM - ROPE_DIM
    c = jnp.concatenate([cos, cos, jnp.ones((n, rest), F32)], axis=1)
    sa = jnp.concatenate([-sin, jnp.zeros((n, half + rest), F32)], axis=1)
    sb = jnp.concatenate([jnp.zeros((n, half), F32), sin, jnp.zeros((n, rest), F32)], axis=1)
    return c, sa, sb


def _overlap_matrix(nch, nsp):
    c = jnp.arange(nch)[:, None]
    s = jnp.arange(nsp)[None, :]
    c_start, c_end = c * CMP_STRIDE, c * CMP_STRIDE + CMP_LEN - 1
    return ((c_start < s * SEL_LEN + SEL_LEN) & (c_end >= s * SEL_LEN)).astype(F32)


def _decay_operands(f_all, q_first, t_q, n_groups):
    b_sz, length, heads = f_all.shape
    rep = heads // n_groups
    fq = f_all[:, q_first:q_first + t_q].reshape(b_sz, t_q, n_groups, rep).transpose(0, 2, 1, 3)
    ft = f_all.reshape(b_sz, length, n_groups, rep).transpose(0, 2, 3, 1)
    return fq, ft


def kernel(x_prompt, x_sample, c_prompt, c_sample, cache_fox_k, cache_fox_v, cache_fox_logf, cache_cmp_k, cache_cmp_v, cache_sel_k, cache_sel_v, state_win_k, state_win_v, page_table, w_ada, b_ada, g_mix, g_ffn, w_in, b_forget, g_q_fox, g_k_fox, g_q_nsa, g_k_cmp, g_k_sel, g_k_win, cmp_pe_k, cmp_pe_v, w_cmp_k1, w_cmp_k2, w_cmp_v1, w_cmp_v2, w_branch, w_out, w_router, router_bias, w_exp_gate, w_exp_up, w_exp_down, w_sh_gate, w_sh_up, w_sh_down):
    hd = HEAD_DIM
    bp, seq, d = x_prompt.shape
    bs, dseq, _ = x_sample.shape
    assert bp == 1
    n_pool, page = cache_fox_k.shape[:2]
    past = page_table.shape[1] * page
    w_len = state_win_k.shape[1]
    gf, gn = FOX_KV_HEADS, NSA_KV_HEADS
    assert gf == gn and FOX_HEADS == NSA_HEADS
    n_p, n_s = bp * seq, bs * dseq
    n_all = n_p + n_s
    kvw = gn * hd
    qw = NSA_HEADS * hd
    page_table = page_table.astype(jnp.int32)

    c_all = jnp.concatenate([c_prompt, c_sample], axis=0)
    n_c = c_all.shape[0]
    n_cp = _round_up(n_c, SUBLANES)
    c_all = jnp.pad(c_all, ((0, n_cp - n_c), (0, 0)))
    mod = _matmul(c_all, w_ada, tm=n_cp, tn=_tile(6 * d, 2048, LANES), tk=_tile(d, 512, LANES),
                  out_dtype=F32, name="ada", prologue=_silu,
                  extras=(b_ada.reshape(1, 6 * d),),
                  extra_specs=(pl.BlockSpec((1, _tile(6 * d, 2048, LANES)), lambda i, j, k: (0, j)),),
                  epilogue=lambda r, b: r + b)
    sh1, sc1, ga1, sh2, sc2, ga2 = [mod[:, i * d:(i + 1) * d] for i in range(6)]
    per_row = lambda v: jnp.repeat(v[bp:bp + bs], dseq, axis=0)

    xp2, xs2 = x_prompt.reshape(n_p, d), x_sample.reshape(n_s, d)
    tt_p = _tile(n_p, 256, 16)
    h_p = _norm_mod(xp2, g_mix, sc1[:1], sh1[:1], tt=tt_p, name="norm1_prompt")
    h_s = _norm_mod(xs2, g_mix, per_row(sc1), per_row(sh1), tt=n_s, name="norm1_sample")
    h_all = jnp.concatenate([h_p, h_s], axis=0)

    fw, nw = FOX_HEADS * hd, NSA_HEADS * hd
    kvf = gf * hd
    sizes = [fw, kvf, kvf, FOX_HEADS, nw, kvw, kvw, kvw, kvw, kvw, kvw, 3 * NSA_HEADS, N_BRANCH * d]
    offs = [0]
    for s_ in sizes:
        offs.append(offs[-1] + s_)
    seg = lambda i: w_in[:, offs[i]:offs[i + 1]]
    order = [0, 4, 1, 2, 5, 6, 7, 8, 9, 10, 12, 3, 11]
    assert FOX_HEADS + 3 * NSA_HEADS <= LANES
    np_cols = sum(sizes[i] for i in order[:-2]) + LANES
    tn_in = _tile(_round_up(np_cols, 1280), 1280, LANES) if np_cols > 1280 else np_cols
    np_pad = _round_up(np_cols, tn_in)
    w_in_p = jnp.concatenate([seg(i) for i in order]
                             + [jnp.zeros((d, np_pad - np_cols + LANES - FOX_HEADS - 3 * NSA_HEADS), w_in.dtype)],
                             axis=1).astype(BF16)
    col = {}
    acc_ = 0
    for i in order[:-2]:
        col[i] = acc_
        acc_ += sizes[i]
    col_gates = acc_
    tm_all = _tile(n_all, 1664, 16)
    proj = _matmul(h_all, w_in_p, tm=tm_all, tn=tn_in, tk=_tile(d, 1024, LANES), out_dtype=F32, name="in_proj")

    pos_all = jnp.concatenate([jnp.arange(seq, dtype=jnp.int32)] * bp
                              + [past + jnp.arange(dseq, dtype=jnp.int32)] * bs)
    tabs = _rope_tables(pos_all)
    tr = _tile(n_all, 320, SUBLANES)
    qf = _head_prep(proj, col[0], FOX_HEADS, g_q_fox, None, tr=tr, name="prep_qf")
    kf = _head_prep(proj, col[1], gf, g_k_fox, None, tr=tr, name="prep_kf")
    qn = _head_prep(proj, col[4], NSA_HEADS, g_q_nsa, tabs, tr=tr, name="prep_qn")
    kc = _head_prep(proj, col[5], gn, None, tabs, tr=tr, name="prep_kc")
    ks = _head_prep(proj, col[7], gn, g_k_sel, tabs, tr=tr, name="prep_ks")
    kw = _head_prep(proj, col[9], gn, g_k_win, tabs, tr=tr, name="prep_kw")
    vf = proj[:, col[2]:col[2] + kvf]
    vc = proj[:, col[6]:col[6] + kvw]
    vs = proj[:, col[8]:col[8] + kvw]
    vw = proj[:, col[10]:col[10] + kvw]
    gates = _gates(proj, col_gates, b_forget, tr=tr, name="gates")
    logf = gates[:, :FOX_HEADS]

    ident = lambda nb, n: jnp.arange(nb * n, dtype=jnp.int32).reshape(nb, n)
    pps = PAGES_PER_STEP

    def cmp_weights(pe, w1, w2):
        hid = w1.shape[1]
        halfk = CMP_STRIDE * hd
        assert CMP_LEN == 2 * CMP_STRIDE
        w1ab = jnp.concatenate([w1[:halfk], w1[halfk:]], axis=1).astype(BF16)
        pe_row = jnp.pad(pe.reshape(1, CMP_LEN * hd), ((0, SUBLANES - 1), (0, 0)))
        cpe = _matmul(pe_row, w1, tm=SUBLANES, tn=hid, tk=_tile(CMP_LEN * hd, 1024, LANES),
                      out_dtype=F32, name="cmp_pe")[:1]
        return w1ab, w1[halfk:], cpe, w2.astype(BF16)

    wk = cmp_weights(cmp_pe_k, w_cmp_k1, w_cmp_k2)
    wv = cmp_weights(cmp_pe_v, w_cmp_v1, w_cmp_v2)
    hid_c = w_cmp_k1.shape[1]
    cpp = page // CMP_STRIDE
    chunk_w = CMP_STRIDE * kvw

    def compress(pages, table, weights, gain, new_rows, name):
        w1ab, w1b, cpe, w2 = weights
        nb = table.shape[0]
        papb = _compress_first(pages, table, w1ab, gn, name=name + "_a")
        if new_rows is None:
            pb_next = jnp.zeros((nb, 1, gn * hid_c), F32)
        else:
            t_new = new_rows.shape[1]
            xn = jnp.pad(new_rows.reshape(nb, t_new, gn, hd), ((0, 0), (0, CMP_STRIDE - t_new), (0, 0), (0, 0)))
            xn = xn.transpose(0, 2, 1, 3).reshape(nb * gn, CMP_STRIDE * hd)
            rows_p = _round_up(nb * gn, SUBLANES)
            xn = jnp.pad(xn, ((0, rows_p - nb * gn), (0, 0)))
            pb_next = _matmul(xn, w1b, tm=rows_p, tn=hid_c, tk=_tile(CMP_STRIDE * hd, 1024, LANES),
                              out_dtype=F32, name=name + "_new")[:nb * gn].reshape(nb, 1, gn * hid_c)
        return _compress_second(papb, pb_next, cpe, w2, gain, gn, name=name + "_b")

    n_pg_p = seq // page
    tab_p = ident(bp, n_pg_p)
    tq_p = _tile(seq, 256, 16)
    tk_p = _tile(seq, 512, LANES)
    as_pages = lambda a, rows: a[:n_p].reshape(n_p // rows, rows, a.shape[1])
    q_p = lambda a: a[:n_p].reshape(bp, seq, a.shape[1])

    f_p = _cumsum_logf(logf[:n_p].reshape(n_p // page, page * FOX_HEADS // LANES, LANES), tab_p, None,
                       name="cumsum_prompt")
    o_fox_p = _attention(q_p(qf), as_pages(kf, tk_p), as_pages(vf, tk_p), ident(bp, seq // tk_p),
                         tq=tq_p, n_pages=1, q0=0, pos_base=0, fuse_groups=False,
                         decay=_decay_operands(f_p, 0, seq, gf), name="fox_prompt")
    ck_p = compress(kc[:n_p].reshape(n_pg_p, cpp, chunk_w), tab_p, wk, g_k_cmp, None, "cmpk_prompt")
    cv_p = compress(vc[:n_p].reshape(n_pg_p, cpp, chunk_w), tab_p, wv, None, None, "cmpv_prompt")
    nch_p = seq // CMP_STRIDE
    ns_p = -(-seq // SEL_LEN)
    nsp_p = _round_up(ns_p, LANES)
    o_cmp_p, mask_p = _cmp_select(q_p(qn), ck_p, cv_p, _overlap_matrix(nch_p, nsp_p), tq=tq_p, q0=0,
                                  nc_valid=nch_p - CMP_LEN // CMP_STRIDE + 1, n_blocks=ns_p, name="cmp_prompt")
    o_sel_p = _attention(q_p(qn), as_pages(ks, tk_p), as_pages(vs, tk_p), ident(bp, seq // tk_p),
                         tq=tq_p, n_pages=1, q0=0, pos_base=0, fuse_groups=False,
                         block_mask=mask_p, name="sel_prompt")
    wtile = min(WINDOW, seq)
    o_win_p = _attention(q_p(qn), as_pages(kw, wtile), as_pages(vw, wtile), ident(bp, seq // wtile),
                         tq=wtile, n_pages=1, q0=0, pos_base=0, fuse_groups=False,
                         window=WINDOW, band_steps=2, name="win_prompt")

    tq_s = _round_up(dseq, SUBLANES)
    q_s = lambda a: jnp.pad(a[n_p:].reshape(bs, dseq, a.shape[1]), ((0, 0), (0, tq_s - dseq), (0, 0)))
    new_page = lambda a, rows: jnp.pad(a[n_p:].reshape(bs, dseq, a.shape[1]), ((0, 0), (0, rows - dseq), (0, 0)))
    cache2 = lambda c: c.reshape(n_pool, page, -1)

    lf_new = new_page(logf, pps * page).reshape(bs, pps * page * FOX_HEADS // LANES, LANES)
    f_s = _cumsum_logf(cache_fox_logf.astype(F32).reshape(n_pool, page * FOX_HEADS // LANES, LANES),
                       page_table, lf_new, name="cumsum_sample")
    o_fox_s = _attention(q_s(qf), cache2(cache_fox_k), cache2(cache_fox_v), page_table,
                         tq=tq_s, n_pages=pps, q0=past, pos_base=0, fuse_groups=True,
                         extra=(new_page(kf, page), new_page(vf, page)), extra_pos=past,
                         decay=_decay_operands(f_s, past, tq_s, gf), name="fox_sample")
    kc_new = kc[n_p:].reshape(bs, dseq, kvw)
    vc_new = vc[n_p:].reshape(bs, dseq, kvw)
    ck_s = compress(cache_cmp_k.reshape(n_pool, cpp, chunk_w), page_table, wk, g_k_cmp, kc_new, "cmpk_sample")
    cv_s = compress(cache_cmp_v.reshape(n_pool, cpp, chunk_w), page_table, wv, None, vc_new, "cmpv_sample")
    len_s = past + dseq
    nch_s = past // CMP_STRIDE
    n_chunks_s = max(-(-len_s // CMP_STRIDE), CMP_LEN // CMP_STRIDE)
    assert n_chunks_s == nch_s + 1
    ns_s = -(-len_s // SEL_LEN)
    nsp_s = _round_up(max(ns_s, (past + page) // SEL_LEN), LANES)
    o_cmp_s, mask_s = _cmp_select(q_s(qn), ck_s, cv_s, _overlap_matrix(nch_s, nsp_s), tq=tq_s, q0=past,
                                  nc_valid=n_chunks_s - CMP_LEN // CMP_STRIDE + 1, n_blocks=ns_s,
                                  name="cmp_sample")
    o_sel_s = _attention(q_s(qn), cache2(cache_sel_k), cache2(cache_sel_v), page_table,
                         tq=tq_s, n_pages=pps, q0=past, pos_base=0, fuse_groups=True,
                         extra=(new_page(ks, page), new_page(vs, page)), extra_pos=past,
                         block_mask=mask_s, name="sel_sample")
    wpg = _tile(w_len, page, SUBLANES)
    n_wpg = w_len // wpg
    o_win_s = _attention(q_s(qn), state_win_k.reshape(bs * n_wpg, wpg, kvw), state_win_v.reshape(bs * n_wpg, wpg, kvw),
                         ident(bs, n_wpg), tq=tq_s, n_pages=n_wpg, q0=past, pos_base=past - w_len,
                         fuse_groups=True, extra=(new_page(kw, page), new_page(vw, page)), extra_pos=past,
                         window=WINDOW, name="win_sample")

    unpad = lambda a: a[:, :dseq].reshape(n_s, a.shape[2])
    cat = lambda a_p, a_s: jnp.concatenate([a_p.reshape(n_p, -1), unpad(a_s)], axis=0)
    o_nsa = _nsa_combine(cat(o_cmp_p, o_cmp_s), cat(o_sel_p, o_sel_s), cat(o_win_p, o_win_s), gates,
                         tr=_tile(n_all, 320, 16), name="nsa_combine")
    o_fox = cat(o_fox_p, o_fox_s)
    tn_d = _tile(d, 1024, LANES)
    assert col[12] % tn_d == 0
    merged = _merge(o_fox, o_nsa, w_branch, proj, col[12], tm=_tile(n_all, 640, 16), tn=tn_d,
                    tk=_tile(fw, 512, LANES), name="merge")

    resid = lambda r, x_, g_: x_ + g_ * r
    tm_p = _tile(n_p, 1024, 16)
    x1_p = _matmul(merged, w_out, tm=tm_p, tn=tn_d, tk=_tile(d, 1024, LANES), out_dtype=F32, name="out_prompt",
                   m_rows=n_p, extras=(xp2, ga1[:1]),
                   extra_specs=(pl.BlockSpec((tm_p, tn_d), lambda i, j, k: (i, j)),
                                pl.BlockSpec((1, tn_d), lambda i, j, k: (0, j))), epilogue=resid)
    x1_s = _matmul(merged, w_out, tm=n_s, tn=tn_d, tk=_tile(d, 1024, LANES), out_dtype=F32, name="out_sample",
                   m_rows=n_s, a_row_off=n_p, extras=(xs2, per_row(ga1)),
                   extra_specs=(pl.BlockSpec((n_s, tn_d), lambda i, j, k: (i, j)),
                                pl.BlockSpec((n_s, tn_d), lambda i, j, k: (i, j))), epilogue=resid)

    n_exp = w_router.shape[1]
    wr_t = w_router.T.astype(F32)
    pk_p, lg_p = _norm_mod(x1_p, g_ffn, sc2[:1], sh2[:1], tt=tt_p, name="norm2_prompt", router_t=wr_t)
    pk_s, lg_s = _norm_mod(x1_s, g_ffn, per_row(sc2), per_row(sh2), tt=n_s, name="norm2_sample", router_t=wr_t)
    packed = jnp.concatenate([pk_p, pk_s], axis=0)
    logits_t = jnp.concatenate([lg_p, lg_s], axis=1)
    n_padr = _round_up(n_all, LANES)
    eidx_t, gw_t = _route(jnp.pad(logits_t, ((0, 0), (0, n_padr - n_all))), router_bias,
                          tn=_tile(n_padr, 640, LANES), name="route")
    eidx_t, gw_t = eidx_t[:, :n_all], gw_t[:, :n_all]
    e_of, nb, tok, dst, n_super = _dispatch_tables(eidx_t, n_exp)
    n_assign = n_all * TOP_K
    ya = _experts(packed, e_of, nb, tok, dst, w_exp_gate, w_exp_up, w_exp_down, n_super, n_assign, name="experts")
    ya = ya[:n_assign].reshape(n_all, TOP_K * d)
    gw = gw_t.T
    hs = _shared_hidden(packed, w_sh_gate, w_sh_up, tm=_tile(n_all, 640, 16),
                        tn=_tile(w_sh_gate.shape[1], 256, LANES), name="shared_hidden")

    def final(r, x_, g_, w_, *ys):
        routed = ys[0] * w_[:, 0:1]
        for k_ in range(1, TOP_K):
            routed = routed + ys[k_] * w_[:, k_:k_ + 1]
        return x_ + g_ * (r + routed)

    def final_call(m_rows, row_off, tm, x1, gate, gate_rows, name):
        ro = row_off // tm
        g_spec = (pl.BlockSpec((tm, tn_d), lambda i, j, k: (i, j)) if gate_rows
                  else pl.BlockSpec((1, tn_d), lambda i, j, k: (0, j)))
        y_specs = tuple(pl.BlockSpec((tm, tn_d), functools.partial(lambda i, j, k, kk: (i + ro, kk * (d // tn_d) + j), kk=kk))
                        for kk in range(TOP_K))
        return _matmul(hs, w_sh_down, tm=tm, tn=tn_d, tk=_tile(w_sh_down.shape[0], 1024, LANES), out_dtype=F32,
                       name=name, m_rows=m_rows, a_row_off=row_off,
                       extras=(x1, gate, gw) + (ya,) * TOP_K,
                       extra_specs=(pl.BlockSpec((tm, tn_d), lambda i, j, k: (i, j)), g_spec,
                                    pl.BlockSpec((tm, TOP_K), lambda i, j, k: (i + ro, 0))) + y_specs,
                       epilogue=final)

    tm_f = _tile(n_p, 256, 16)
    y_p = final_call(n_p, 0, tm_f, x1_p, ga2[:1], False, "final_prompt")
    assert n_p % n_s == 0
    y_s = final_call(n_s, n_p, n_s, x1_s, per_row(ga2), True, "final_sample")

    heads4 = lambda a, lo, hi, b_, t_, g_: a[lo:hi].reshape(b_, t_, g_, hd)
    wl_p = min(WINDOW, seq)
    prompt_state = (heads4(kf, 0, n_p, bp, seq, gf), heads4(vf, 0, n_p, bp, seq, gf),
                    logf[:n_p].reshape(bp, seq, FOX_HEADS),
                    heads4(kc, 0, n_p, bp, seq, gn), heads4(vc, 0, n_p, bp, seq, gn),
                    heads4(ks, 0, n_p, bp, seq, gn), heads4(vs, 0, n_p, bp, seq, gn),
                    heads4(kw, 0, n_p, bp, seq, gn)[:, -wl_p:], heads4(vw, 0, n_p, bp, seq, gn)[:, -wl_p:])
    s4 = lambda a, g_: heads4(a, n_p, n_all, bs, dseq, g_)
    wl_s = min(WINDOW, w_len + dseq)
    win_cat = lambda st, new: jnp.concatenate([st, new.astype(st.dtype)], axis=1)[:, -wl_s:]
    sample_state = (s4(kf, gf), s4(vf, gf), logf[n_p:].reshape(bs, dseq, FOX_HEADS),
                    s4(kc, gn), s4(vc, gn), s4(ks, gn), s4(vs, gn),
                    win_cat(state_win_k, s4(kw, gn)), win_cat(state_win_v, s4(vw, gn)))
    return (y_p.reshape(bp, seq, d), y_s.reshape(bs, dseq, d)) + prompt_state + sample_state
```

```python
import functools

import jax
import jax.numpy as jnp
from jax import lax
from jax.experimental import pallas as pl
from jax.experimental.pallas import tpu as pltpu

F32 = jnp.float32
BF16 = jnp.bfloat16
HIGHEST = lax.Precision.HIGHEST

HEAD_DIM = 128
FOX_HEADS = 16
FOX_KV_HEADS = 4
NSA_HEADS = 16
NSA_KV_HEADS = 4
N_BRANCH = 2
ROPE_THETA = 500000.0
ROPE_DIM = HEAD_DIM // 4
CMP_LEN = 32
CMP_STRIDE = 16
SEL_LEN = 64
N_SEL = 16
WINDOW = 512
N_EXPERT_GROUPS = 8
TOPK_GROUPS = 4
TOP_K = 8
ROUTED_SCALE = 2.5
MOE_BLOCK = 128
EPS = 1e-6
NEG = -1e30
BIG = 1e30
LOWEST = -3.0e38

LANES = 128
SUBLANES = 8
VMEM_LIMIT = 56 * 1024 * 1024
PAGES_PER_STEP = 8
SUPER_BLOCKS = 6
HIDDEN_CHUNK = 256
DMA_UNROLL = 8
LOG2E = 1.4426950408889634


def _tile(n, pref, align):
    t = (min(pref, n) // align) * align
    while t >= align:
        if n % t == 0:
            return t
        t -= align
    return n


def _round_up(n, m):
    return -(-n // m) * m


def _params(sem, vmem=VMEM_LIMIT, **kw):
    return pltpu.CompilerParams(dimension_semantics=sem, vmem_limit_bytes=vmem, **kw)


def _dot(a, b):
    return jnp.dot(a.astype(BF16), b.astype(BF16), preferred_element_type=F32)


def _dot_nt(a, b):
    return lax.dot_general(a.astype(BF16), b.astype(BF16), (((1,), (1,)), ((), ())),
                           preferred_element_type=F32)


def _dot_f32(a, b):
    return jnp.dot(a, b, preferred_element_type=F32, precision=HIGHEST)


def _silu(x):
    return x * (1.0 / (1.0 + jnp.exp(-x)))


def _sigmoid(x):
    return 1.0 / (1.0 + jnp.exp(-x))


def _matmul(a, w, *, tm, tn, tk, out_dtype, name, m_rows=None, a_row_off=0,
            extras=(), extra_specs=(), prologue=None, epilogue=None, j_outer=False):
    m = m_rows if m_rows is not None else a.shape[0]
    k_dim, n = w.shape
    assert m % tm == 0 and n % tn == 0 and k_dim % tk == 0 and a_row_off % tm == 0
    nk = k_dim // tk
    ne = len(extras)
    roff = a_row_off // tm

    def body(a_ref, w_ref, *refs):
        ex, o_ref, acc = refs[:ne], refs[ne], refs[ne + 1]
        k = pl.program_id(2)
        av = a_ref[...]
        if prologue is not None:
            av = prologue(av)
        part = _dot(av, w_ref[...])

        @pl.when(k == 0)
        def _():
            acc[...] = part

        @pl.when(k > 0)
        def _():
            acc[...] += part

        @pl.when(k == nk - 1)
        def _():
            r = acc[...]
            if epilogue is not None:
                r = epilogue(r, *[e[...] for e in ex])
            o_ref[...] = r.astype(out_dtype)

    def spec(shape, fn):
        return pl.BlockSpec(shape, (lambda j, i, k: fn(i, j, k)) if j_outer else fn)

    return pl.pallas_call(
        body,
        out_shape=jax.ShapeDtypeStruct((m, n), out_dtype),
        grid=(n // tn, m // tm, nk) if j_outer else (m // tm, n // tn, nk),
        in_specs=[spec((tm, tk), lambda i, j, k: (i + roff, k)),
                  spec((tk, tn), lambda i, j, k: (k, j))] + [spec(s, f) for s, f in extra_specs],
        out_specs=spec((tm, tn), lambda i, j, k: (i, j)),
        scratch_shapes=[pltpu.VMEM((tm, tn), F32)],
        compiler_params=_params(("parallel", "parallel", "arbitrary")),
        name=name,
    )(a, w, *extras)


def _norm_mod(x, gain, scale, shift, *, tt, name, router_t=None):
    m, d = x.shape
    per_row = scale.shape[0] != 1
    mod_spec = (pl.BlockSpec((tt, d), lambda i: (i, 0)) if per_row
                else pl.BlockSpec((1, d), lambda i: (0, 0)))
    routed = router_t is not None

    def body(x_ref, g_ref, sc_ref, sh_ref, *refs):
        xv = x_ref[...]
        y = xv * lax.rsqrt(jnp.mean(xv * xv, axis=-1, keepdims=True) + EPS) * g_ref[...]
        h = y * (1.0 + sc_ref[...]) + sh_ref[...]
        if not routed:
            refs[0][...] = h.astype(BF16)
            return
        r_ref, p_ref, l_ref = refs
        hb = lax.bitcast_convert_type(h.astype(BF16).astype(F32), jnp.uint32)
        half = d // 2
        p_ref[...] = (hb[:, half:] & jnp.uint32(0xFFFF0000)) | (hb[:, :half] >> 16)
        l_ref[...] = lax.dot_general(r_ref[...], h, (((1,), (1,)), ((), ())),
                                     preferred_element_type=F32, precision=HIGHEST)

    in_specs = [pl.BlockSpec((tt, d), lambda i: (i, 0)), pl.BlockSpec((1, d), lambda i: (0, 0)),
                mod_spec, mod_spec]
    args = [x, gain.reshape(1, d), scale, shift]
    if routed:
        e = router_t.shape[0]
        in_specs.append(pl.BlockSpec((e, d), lambda i: (0, 0)))
        args.append(router_t)
        out_shape = (jax.ShapeDtypeStruct((m, d // 2), jnp.uint32), jax.ShapeDtypeStruct((e, m), F32))
        out_specs = (pl.BlockSpec((tt, d // 2), lambda i: (i, 0)), pl.BlockSpec((e, tt), lambda i: (0, i)))
    else:
        out_shape = jax.ShapeDtypeStruct((m, d), BF16)
        out_specs = pl.BlockSpec((tt, d), lambda i: (i, 0))
    return pl.pallas_call(body, out_shape=out_shape, grid=(m // tt,), in_specs=in_specs,
                          out_specs=out_specs, compiler_params=_params(("parallel",)), name=name)(*args)


def _head_prep(proj, col0, n_heads, gain, rope_tabs, *, tr, name):
    rows = proj.shape[0]
    width = n_heads * HEAD_DIM
    assert col0 % width == 0
    do_norm, do_rope = gain is not None, rope_tabs is not None
    half = ROPE_DIM // 2

    def body(*refs):
        x_ref = refs[0]
        pos = 1
        if do_norm:
            g_ref = refs[pos]
            pos += 1
        if do_rope:
            c_ref, sa_ref, sb_ref = refs[pos:pos + 3]
            pos += 3
        o_ref = refs[pos]
        for h in range(n_heads):
            x = x_ref[:, h * HEAD_DIM:(h + 1) * HEAD_DIM]
            if do_norm:
                x = x * lax.rsqrt(jnp.mean(x * x, axis=-1, keepdims=True) + EPS) * g_ref[...]
            if do_rope:
                x = (x * c_ref[...] + pltpu.roll(x, HEAD_DIM - half, 1) * sa_ref[...]
                     + pltpu.roll(x, half, 1) * sb_ref[...])
            o_ref[:, h * HEAD_DIM:(h + 1) * HEAD_DIM] = x

    in_specs = [pl.BlockSpec((tr, width), lambda i: (i, col0 // width))]
    args = [proj]
    if do_norm:
        in_specs.append(pl.BlockSpec((1, HEAD_DIM), lambda i: (0, 0)))
        args.append(gain.reshape(1, HEAD_DIM))
    if do_rope:
        in_specs += [pl.BlockSpec((tr, HEAD_DIM), lambda i: (i, 0))] * 3
        args += list(rope_tabs)
    return pl.pallas_call(body, out_shape=jax.ShapeDtypeStruct((rows, width), F32),
                          grid=(rows // tr,), in_specs=in_specs,
                          out_specs=pl.BlockSpec((tr, width), lambda i: (i, 0)),
                          compiler_params=_params(("parallel",)), name=name)(*args)


def _gates(proj, col0, b_forget, *, tr, name):
    rows = proj.shape[0]
    bias = jnp.zeros((1, LANES), F32).at[0, :FOX_HEADS].set(b_forget.astype(F32))

    def body(x_ref, b_ref, o_ref):
        x = x_ref[...] + b_ref[...]
        lane = lax.broadcasted_iota(jnp.int32, x.shape, 1)
        logsig = jnp.minimum(x, 0.0) - jnp.log(1.0 + jnp.exp(-jnp.abs(x)))
        o_ref[...] = jnp.where(lane < FOX_HEADS, logsig, _sigmoid(x))

    return pl.pallas_call(body, out_shape=jax.ShapeDtypeStruct((rows, LANES), F32),
                          grid=(rows // tr,),
                          in_specs=[pl.BlockSpec((tr, LANES), lambda i: (i, col0 // LANES)),
                                    pl.BlockSpec((1, LANES), lambda i: (0, 0))],
                          out_specs=pl.BlockSpec((tr, LANES), lambda i: (i, 0)),
                          compiler_params=_params(("parallel",)), name=name)(proj, bias)


def _page_specs(n_pages, block_tail, tail_index, page_of_step):
    def make(j):
        def index(*a):
            *gi, pt = a
            b = gi[0]
            return (pt[b, page_of_step(*gi) + j],) + tuple(tail_index(*gi))
        return pl.BlockSpec((1,) + tuple(block_tail), index)
    return [make(j) for j in range(n_pages)]


def _cumsum_logf(pages, page_table, extra, *, name):
    b_sz, n_pg = page_table.shape
    p = PAGES_PER_STEP
    assert n_pg % p == 0 and LANES % FOX_HEADS == 0
    rp = pages.shape[1]
    rows = p * rp
    nst = n_pg // p
    has_x = extra is not None
    steps = nst + (1 if has_x else 0)

    def body(pt_ref, *refs):
        prefs = refs[:p]
        x_ref = refs[p] if has_x else None
        o_ref, carry = refs[p + has_x], refs[p + has_x + 1]
        c = pl.program_id(1)

        @pl.when(c == 0)
        def _():
            carry[...] = jnp.zeros_like(carry)

        def run(x):
            ri = lax.broadcasted_iota(jnp.int32, (LANES, LANES), 0)
            ci = lax.broadcasted_iota(jnp.int32, (LANES, LANES), 1)
            same = (ri % FOX_HEADS) == (ci % FOX_HEADS)
            within = (same & (ri // FOX_HEADS <= ci // FOX_HEADS)).astype(F32)
            total = _dot_f32(x, same.astype(F32))
            rr = lax.broadcasted_iota(jnp.int32, (rows, rows), 0)
            rc = lax.broadcasted_iota(jnp.int32, (rows, rows), 1)
            before = _dot_f32((rc < rr).astype(F32), total)
            o_ref[0] = _dot_f32(x, within) + before + carry[...]
            carry[...] += jnp.sum(total, axis=0, keepdims=True)

        if has_x:
            @pl.when(c < nst)
            def _():
                run(jnp.concatenate([r[0] for r in prefs], axis=0))

            @pl.when(c == nst)
            def _():
                run(x_ref[0])
        else:
            run(jnp.concatenate([r[0] for r in prefs], axis=0))

    in_specs = _page_specs(p, (rp, LANES), lambda b, c: (0, 0),
                           lambda b, c: jnp.minimum(c, nst - 1) * p)
    args = [pages] * p
    if has_x:
        in_specs.append(pl.BlockSpec((1, rows, LANES), lambda b, c, pt: (b, 0, 0)))
        args.append(extra)
    out = pl.pallas_call(
        body,
        out_shape=jax.ShapeDtypeStruct((b_sz, steps * rows, LANES), F32),
        grid_spec=pltpu.PrefetchScalarGridSpec(
            num_scalar_prefetch=1, grid=(b_sz, steps), in_specs=in_specs,
            out_specs=pl.BlockSpec((1, rows, LANES), lambda b, c, pt: (b, c, 0)),
            scratch_shapes=[pltpu.VMEM((1, LANES), F32)]),
        compiler_params=_params(("parallel", "arbitrary")), name=name,
    )(page_table, *args)
    return out.reshape(b_sz, steps * rows * LANES // FOX_HEADS, FOX_HEADS)


def _compress_first(pages, page_table, w1ab, n_groups, *, name):
    b_sz, n_pg = page_table.shape
    p = PAGES_PER_STEP
    assert n_pg % p == 0
    cpp = pages.shape[1] // (CMP_STRIDE * n_groups)
    n2 = w1ab.shape[1]
    rows = p * cpp
    hop = CMP_STRIDE * n_groups

    def body(pt_ref, *refs):
        prefs, w_ref, o_ref = refs[:p], refs[p], refs[p + 1]

        def piece(g, l):
            return jnp.concatenate([r[0, pl.ds(l * n_groups + g, cpp, stride=hop), :] for r in prefs], axis=0)

        lhs = jnp.concatenate(
            [jnp.concatenate([piece(g, l) for l in range(CMP_STRIDE)], axis=1) for g in range(n_groups)], axis=0)
        res = _dot(lhs, w_ref[...])
        for g in range(n_groups):
            o_ref[0, :, g * n2:(g + 1) * n2] = res[g * rows:(g + 1) * rows]

    in_specs = _page_specs(p, pages.shape[1:], lambda b, c: (0, 0), lambda b, c: c * p)
    in_specs.append(pl.BlockSpec(w1ab.shape, lambda b, c, pt: (0, 0)))
    return pl.pallas_call(
        body,
        out_shape=jax.ShapeDtypeStruct((b_sz, n_pg * cpp, n_groups * n2), F32),
        grid_spec=pltpu.PrefetchScalarGridSpec(
            num_scalar_prefetch=1, grid=(b_sz, n_pg // p), in_specs=in_specs,
            out_specs=pl.BlockSpec((1, rows, n_groups * n2), lambda b, c, pt: (b, c, 0))),
        compiler_params=_params(("parallel", "parallel")), name=name,
    )(page_table, *([pages] * p), w1ab)


def _compress_second(papb, pb_next, cpe, w2, gain, n_groups, *, name):
    b_sz, nch, _ = papb.shape
    hid = w2.shape[0]
    do_norm = gain is not None
    g_arr = (gain if do_norm else jnp.ones((HEAD_DIM,), F32)).reshape(1, HEAD_DIM)

    def body(x_ref, n_ref, c_ref, w_ref, g_ref, o_ref):
        row = lax.broadcasted_iota(jnp.int32, (nch, hid), 0)
        for g in range(n_groups):
            pa = x_ref[0, :, g * 2 * hid: g * 2 * hid + hid]
            pb = x_ref[0, :, g * 2 * hid + hid:(g + 1) * 2 * hid]
            nxt = jnp.where(row == nch - 1, n_ref[0, :, g * hid:(g + 1) * hid],
                            pltpu.roll(pb, nch - 1, 0))
            pre = pa + nxt + c_ref[...]
            act = 0.5 * pre * (1.0 + jnp.tanh(0.7978845608028654 * (pre + 0.044715 * pre * pre * pre)))
            y = _dot(act, w_ref[...])
            if do_norm:
                y = y * lax.rsqrt(jnp.mean(y * y, axis=-1, keepdims=True) + EPS) * g_ref[...]
            o_ref[0, :, g * HEAD_DIM:(g + 1) * HEAD_DIM] = y

    return pl.pallas_call(
        body, out_shape=jax.ShapeDtypeStruct((b_sz, nch, n_groups * HEAD_DIM), F32), grid=(b_sz,),
        in_specs=[pl.BlockSpec((1, nch, papb.shape[2]), lambda b: (b, 0, 0)),
                  pl.BlockSpec((1, 1, n_groups * hid), lambda b: (b, 0, 0)),
                  pl.BlockSpec((1, hid), lambda b: (0, 0)),
                  pl.BlockSpec(w2.shape, lambda b: (0, 0)),
                  pl.BlockSpec((1, HEAD_DIM), lambda b: (0, 0))],
        out_specs=pl.BlockSpec((1, nch, n_groups * HEAD_DIM), lambda b: (b, 0, 0)),
        compiler_params=_params(("parallel",)), name=name,
    )(papb, pb_next, cpe, w2, g_arr)


def _cmp_select(q, ck, cv, overlap, *, tq, q0, nc_valid, n_blocks, name):
    b_sz, t_q, hw = q.shape
    nch = ck.shape[1]
    n_groups = ck.shape[2] // HEAD_DIM
    rep = hw // HEAD_DIM // n_groups
    nsp = overlap.shape[1]
    n_keep = min(N_SEL, n_blocks)
    scale = HEAD_DIM ** -0.5

    def body(q_ref, k_ref, v_ref, ov_ref, o_ref, m_ref):
        qi = pl.program_id(2)
        qv = q_ref[0]
        q4 = jnp.concatenate([qv[:, r * HEAD_DIM:(r + 1) * HEAD_DIM] for r in range(rep)], axis=0)
        s = _dot_nt(q4, k_ref[0]) * scale
        qpos = q0 + qi * tq + lax.broadcasted_iota(jnp.int32, (tq, 1), 0)
        cidx = lax.broadcasted_iota(jnp.int32, (1, nch), 1)
        cmask = (cidx * CMP_STRIDE + (CMP_LEN - 1) <= qpos) & (cidx < nc_valid)
        s3 = jnp.where(cmask[None], s.reshape(rep, tq, nch), NEG)
        mx = jnp.max(s3, axis=-1, keepdims=True)
        e = jnp.where(cmask[None], jnp.exp(s3 - mx), 0.0)
        den = jnp.sum(e, axis=-1, keepdims=True)
        p3 = e / jnp.where(den > 0.0, den, 1.0)
        o = _dot(p3.reshape(rep * tq, nch), v_ref[0])
        for r in range(rep):
            o_ref[0, :, r * HEAD_DIM:(r + 1) * HEAD_DIM] = o[r * tq:(r + 1) * tq].astype(BF16)
        imp = _dot_f32(jnp.sum(p3, axis=0), ov_ref[...])
        blk = lax.broadcasted_iota(jnp.int32, (1, nsp), 1)
        cur = qpos // SEL_LEN
        valid = (blk * SEL_LEN <= qpos) & (blk < n_blocks)
        forced = (blk == 0) | (blk == cur) | (blk == cur - 1)
        score = jnp.where(valid & forced, BIG, jnp.where(valid, imp, NEG))
        keep = jnp.zeros((tq, nsp), jnp.bool_)
        for _ in range(n_keep):
            best = jnp.max(score, axis=-1, keepdims=True)
            first = jnp.min(jnp.where(score == best, blk, nsp), axis=-1, keepdims=True)
            hit = blk == first
            keep = keep | hit
            score = jnp.where(hit, LOWEST, score)
        m_ref[0] = jnp.where(keep & valid, 1.0, 0.0).astype(BF16)

    gw = rep * HEAD_DIM
    return pl.pallas_call(
        body,
        out_shape=(jax.ShapeDtypeStruct((b_sz, t_q, hw), BF16),
                   jax.ShapeDtypeStruct((b_sz, t_q, n_groups * nsp), BF16)),
        grid=(b_sz, n_groups, t_q // tq),
        in_specs=[pl.BlockSpec((1, tq, gw), lambda b, g, i: (b, i, g)),
                  pl.BlockSpec((1, nch, HEAD_DIM), lambda b, g, i: (b, 0, g)),
                  pl.BlockSpec((1, nch, HEAD_DIM), lambda b, g, i: (b, 0, g)),
                  pl.BlockSpec(overlap.shape, lambda b, g, i: (0, 0))],
        out_specs=(pl.BlockSpec((1, tq, gw), lambda b, g, i: (b, i, g)),
                   pl.BlockSpec((1, tq, nsp), lambda b, g, i: (b, i, g))),
        compiler_params=_params(("parallel", "parallel", "parallel")), name=name,
    )(q, ck, cv, overlap)


def _attention(q, ksrc, vsrc, page_table, *, tq, n_pages, q0, pos_base, name, fuse_groups,
               extra=None, extra_pos=0, decay=None, block_mask=None, window=None,
               band_steps=None):
    b_sz, t_q, hw = q.shape
    pg = ksrc.shape[1]
    n_groups = ksrc.shape[2] // HEAD_DIM
    rep = hw // HEAD_DIM // n_groups
    n_pg = page_table.shape[1]
    p = n_pages
    assert n_pg % p == 0 and t_q % tq == 0
    nst = n_pg // p
    tk = p * pg
    banded = band_steps is not None
    if banded:
        assert p == 1 and tq == pg and window == pg and extra is None
    has_x = extra is not None
    main_steps = band_steps if banded else nst
    steps = main_steps + (1 if has_x else 0)
    scale = HEAD_DIM ** -0.5
    has_decay, has_bm = decay is not None, block_mask is not None
    nsp = block_mask.shape[2] // n_groups if has_bm else 0
    groups = range(n_groups) if fuse_groups else (None,)
    ng_blk = n_groups if fuse_groups else 1
    kw_blk = ng_blk * HEAD_DIM
    qw_blk = ng_blk * rep * HEAD_DIM

    def norm(gi):
        if fuse_groups:
            b, i, c = gi
            return b, 0, i, c
        return gi

    def last_step(i):
        return jnp.clip((q0 + (i + 1) * tq - 1 - pos_base) // tk, 0, nst - 1)

    def first_page(*gi):
        b, g, i, c = norm(gi)
        if banded:
            return jnp.maximum(i - (band_steps - 1) + c, 0)
        return jnp.minimum(c, last_step(i)) * p

    def body(pt_ref, *refs):
        pos = 0
        q_ref = refs[pos]; pos += 1
        k_refs = refs[pos:pos + p]; pos += p
        v_refs = refs[pos:pos + p]; pos += p
        if has_x:
            kx_ref, vx_ref = refs[pos:pos + 2]; pos += 2
        if has_decay:
            ft_ref = refs[pos]; pos += 1
        if has_bm:
            bm_ref = refs[pos]; pos += 1
        o_ref, m_scr, l_scr, a_scr = refs[pos:pos + 4]
        i = pl.program_id(1 if fuse_groups else 2)
        c = pl.program_id(2 if fuse_groups else 3)

        @pl.when(c == 0)
        def _():
            m_scr[...] = jnp.full(m_scr.shape, NEG, F32)
            l_scr[...] = jnp.zeros(l_scr.shape, F32)
            a_scr[...] = jnp.zeros(a_scr.shape, F32)

        qpos = q0 + i * tq + lax.broadcasted_iota(jnp.int32, (tq, 1), 0)

        def tile(gi, kt, vt, kpos, ft_t, masked=True):
            go = 0 if gi is None else gi
            n = kt.shape[0]
            qv = q_ref[0]
            q4 = jnp.concatenate([qv[:, (go * rep + r) * HEAD_DIM:(go * rep + r + 1) * HEAD_DIM]
                                  for r in range(rep)], axis=0)
            s3 = (_dot_nt(q4, kt) * (scale * LOG2E)).reshape(rep, tq, n)
            if has_decay:
                s3 = s3 - ft_t[:, None, :]
            if masked:
                vis = kpos <= qpos
                if window is not None:
                    vis = vis & (qpos - kpos < window) & (kpos >= pos_base)
                if has_bm:
                    sel = bm_ref[0][:, go * nsp:(go + 1) * nsp]
                    blk = lax.broadcasted_iota(jnp.int32, (nsp, n), 0)
                    expand = (blk == kpos // SEL_LEN).astype(BF16)
                    vis = vis & (jnp.dot(sel, expand, preferred_element_type=F32) > 0.5)
                s3 = jnp.where(vis[None], s3, NEG)
            m_old = m_scr[go]
            m_new = jnp.maximum(m_old, jnp.max(s3, axis=-1, keepdims=True))
            alpha = jnp.exp2(m_old - m_new)
            pr = jnp.exp2(s3 - jnp.concatenate([m_new] * (n // LANES), axis=-1))
            l_scr[go] = alpha * l_scr[go] + jnp.sum(pr, axis=-1, keepdims=True)
            pv = _dot(pr.reshape(rep * tq, n), vt).reshape(rep, tq, HEAD_DIM)
            a_scr[go] = alpha * a_scr[go] + pv
            m_scr[go] = m_new

        def main(masked=True):
            if banded:
                start = (i - (band_steps - 1) + c) * pg + pos_base
            else:
                start = c * tk + pos_base
            kpos = start + lax.broadcasted_iota(jnp.int32, (1, tk), 1)
            for gi in groups:
                go = 0 if gi is None else gi
                lo, hi = go * HEAD_DIM, (go + 1) * HEAD_DIM
                kt = jnp.concatenate([r[0][:, lo:hi] for r in k_refs], axis=0)
                vt = jnp.concatenate([r[0][:, lo:hi] for r in v_refs], axis=0)
                ft_t = ft_ref[0, go] if has_decay else None
                tile(gi, kt, vt, kpos, ft_t, masked)

        if banded:
            main()
        else:
            needed = c <= last_step(i)
            if has_x:
                needed = needed & (c < nst)
            if has_bm or window is not None:
                pl.when(needed)(main)
            else:
                clear = (c + 1) * tk + pos_base - 1 <= q0 + i * tq
                pl.when(needed & clear)(functools.partial(main, False))
                pl.when(needed & jnp.logical_not(clear))(main)

        if has_x:
            @pl.when(c == nst)
            def _():
                nx = kx_ref.shape[1]
                kpos = extra_pos + lax.broadcasted_iota(jnp.int32, (1, nx), 1)
                for gi in groups:
                    go = 0 if gi is None else gi
                    lo, hi = go * HEAD_DIM, (go + 1) * HEAD_DIM
                    ft_t = ft_ref[0, go][:, :nx] if has_decay else None
                    tile(gi, kx_ref[0][:, lo:hi], vx_ref[0][:, lo:hi], kpos, ft_t)

        @pl.when(c == steps - 1)
        def _():
            for gi in groups:
                go = 0 if gi is None else gi
                out = a_scr[go] / l_scr[go]
                for r in range(rep):
                    col = (go * rep + r) * HEAD_DIM
                    o_ref[0, :, col:col + HEAD_DIM] = out[r].astype(BF16)

    def gsel(gi):
        b, g, i, c = norm(gi)
        return b, g, i, c

    in_specs = [pl.BlockSpec((1, tq, qw_blk), lambda *a: (gsel(a[:-1])[0], gsel(a[:-1])[2], gsel(a[:-1])[1]))]
    args = [q]
    kv_tail = lambda *gi: (0, norm(gi)[1])
    in_specs += _page_specs(p, (pg, kw_blk), kv_tail, first_page)
    in_specs += _page_specs(p, (pg, kw_blk), kv_tail, first_page)
    args += [ksrc] * p + [vsrc] * p
    if has_x:
        pgx = extra[0].shape[1]
        xs = pl.BlockSpec((1, pgx, kw_blk), lambda *a: (norm(a[:-1])[0], 0, norm(a[:-1])[1]))
        in_specs += [xs, xs]
        args += list(extra)
    if has_decay:
        def ft_index(*a):
            b, g, i, c = norm(a[:-1])
            step = jnp.where(c >= nst, nst, jnp.minimum(c, last_step(i))) if has_x else jnp.minimum(c, last_step(i))
            return (b, g, 0, step)
        in_specs.append(pl.BlockSpec((1, ng_blk, rep, tk), ft_index))
        args.append(decay)
    if has_bm:
        in_specs.append(pl.BlockSpec((1, tq, ng_blk * nsp),
                                     lambda *a: (norm(a[:-1])[0], norm(a[:-1])[2], norm(a[:-1])[1])))
        args.append(block_mask)
    grid = (b_sz, t_q // tq, steps) if fuse_groups else (b_sz, n_groups, t_q // tq, steps)
    sem = ("parallel",) * (len(grid) - 1) + ("arbitrary",)
    return pl.pallas_call(
        body,
        out_shape=jax.ShapeDtypeStruct((b_sz, t_q, hw), BF16),
        grid_spec=pltpu.PrefetchScalarGridSpec(
            num_scalar_prefetch=1, grid=grid, in_specs=in_specs,
            out_specs=pl.BlockSpec((1, tq, qw_blk),
                                   lambda *a: (norm(a[:-1])[0], norm(a[:-1])[2], norm(a[:-1])[1])),
            scratch_shapes=[pltpu.VMEM((ng_blk, rep, tq, LANES), F32), pltpu.VMEM((ng_blk, rep, tq, LANES), F32),
                            pltpu.VMEM((ng_blk, rep, tq, HEAD_DIM), F32)]),
        compiler_params=_params(sem), name=name,
    )(page_table, *args)


def _attention_dec(q, ksrc, vsrc, page_table, extra, *, tq, n_groups, n_pages, q0, pos_base, extra_pos,
                   name, decay=None, block_mask=None, window=None):
    b_sz, rows, _ = q.shape
    pgg = ksrc.shape[1]
    p = n_pages
    n_pg = page_table.shape[1]
    assert n_pg % p == 0 and pgg % n_groups == 0 and rows % (n_groups * tq) == 0
    nst = n_pg // p
    tk = p * pgg
    heads = rows // tq
    rep = heads // n_groups
    scale = HEAD_DIM ** -0.5
    has_decay, has_bm = decay is not None, block_mask is not None
    nsp = block_mask.shape[2] if has_bm else 0
    nx = extra[0].shape[1]

    def body(pt_ref, *refs):
        pos = 0
        q_ref = refs[pos]; pos += 1
        k_refs = refs[pos:pos + p]; pos += p
        v_refs = refs[pos:pos + p]; pos += p
        kx_ref, vx_ref = refs[pos:pos + 2]; pos += 2
        if has_decay:
            ft_ref = refs[pos]; pos += 1
        if has_bm:
            bm_ref = refs[pos]; pos += 1
        o_ref, m_scr, l_scr, a_scr = refs[pos:pos + 4]
        c = pl.program_id(1)

        @pl.when(c == 0)
        def _():
            m_scr[...] = jnp.full(m_scr.shape, NEG, F32)
            l_scr[...] = jnp.zeros(l_scr.shape, F32)
            a_scr[...] = jnp.zeros(a_scr.shape, F32)

        row = lax.broadcasted_iota(jnp.int32, (rows, 1), 0)
        g_row = row // (rep * tq)
        qpos = q0 + row % tq

        def tile(kt, vt, start, ft_t):
            n = kt.shape[0]
            col = lax.broadcasted_iota(jnp.int32, (1, n), 1)
            kpos = start + col // n_groups
            s = _dot_nt(q_ref[0], kt) * (scale * LOG2E)
            if has_decay:
                s = s - jnp.broadcast_to(ft_t[:, None, :], (heads, tq, n)).reshape(rows, n)
            vis = (col % n_groups == g_row) & (kpos <= qpos)
            if window is not None:
                vis = vis & (qpos - kpos < window)
            if has_bm:
                blk = lax.broadcasted_iota(jnp.int32, (nsp, n), 0)
                expand = (blk == kpos // SEL_LEN).astype(BF16)
                hit = jnp.dot(bm_ref[0].astype(BF16), expand, preferred_element_type=F32)
                hit = jnp.broadcast_to(hit.reshape(n_groups, 1, tq, n), (n_groups, rep, tq, n))
                vis = vis & (hit.reshape(rows, n) > 0.5)
            s = jnp.where(vis, s, NEG)
            m_old = m_scr[...]
            m_new = jnp.maximum(m_old, jnp.max(s, axis=-1, keepdims=True))
            alpha = jnp.exp2(m_old - m_new)
            pr = jnp.exp2(s - jnp.concatenate([m_new] * (n // LANES), axis=-1))
            l_scr[...] = alpha * l_scr[...] + jnp.sum(pr, axis=-1, keepdims=True)
            a_scr[...] = alpha * a_scr[...] + _dot(pr, vt)
            m_scr[...] = m_new

        @pl.when(c < nst)
        def _():
            kt = jnp.concatenate([r[0] for r in k_refs], axis=0)
            vt = jnp.concatenate([r[0] for r in v_refs], axis=0)
            tile(kt, vt, pos_base + c * (tk // n_groups), ft_ref[0] if has_decay else None)

        @pl.when(c == nst)
        def _():
            tile(kx_ref[0], vx_ref[0], extra_pos, ft_ref[0][:, :nx] if has_decay else None)
            o_ref[0] = (a_scr[...] / l_scr[...]).astype(BF16)

    first_page = lambda b, c: jnp.minimum(c, nst - 1) * p
    in_specs = [pl.BlockSpec((1, rows, HEAD_DIM), lambda b, c, pt: (b, 0, 0))]
    in_specs += _page_specs(p, (pgg, HEAD_DIM), lambda b, c: (0, 0), first_page)
    in_specs += _page_specs(p, (pgg, HEAD_DIM), lambda b, c: (0, 0), first_page)
    xs = pl.BlockSpec((1, nx, HEAD_DIM), lambda b, c, pt: (b, 0, 0))
    in_specs += [xs, xs]
    args = [q] + [ksrc] * p + [vsrc] * p + list(extra)
    if has_decay:
        in_specs.append(pl.BlockSpec((1, heads, tk), lambda b, c, pt: (b, 0, c)))
        args.append(decay)
    if has_bm:
        in_specs.append(pl.BlockSpec((1, n_groups * tq, nsp), lambda b, c, pt: (b, 0, 0)))
        args.append(block_mask)
    return pl.pallas_call(
        body,
        out_shape=jax.ShapeDtypeStruct((b_sz, rows, HEAD_DIM), BF16),
        grid_spec=pltpu.PrefetchScalarGridSpec(
            num_scalar_prefetch=1, grid=(b_sz, nst + 1), in_specs=in_specs,
            out_specs=pl.BlockSpec((1, rows, HEAD_DIM), lambda b, c, pt: (b, 0, 0)),
            scratch_shapes=[pltpu.VMEM((rows, LANES), F32), pltpu.VMEM((rows, LANES), F32),
                            pltpu.VMEM((rows, HEAD_DIM), F32)]),
        compiler_params=_params(("parallel", "arbitrary")), name=name,
    )(page_table, *args)


def _nsa_combine(o_cmp, o_sel, o_win, gates, *, tr, name):
    rows, hw = o_cmp.shape

    def body(c_ref, s_ref, w_ref, g_ref, o_ref):
        g = g_ref[...]
        for h in range(NSA_HEADS):
            sl = slice(h * HEAD_DIM, (h + 1) * HEAD_DIM)
            lane = FOX_HEADS + h
            acc = (g[:, lane:lane + 1] * c_ref[:, sl].astype(F32)
                   + g[:, lane + NSA_HEADS:lane + NSA_HEADS + 1] * s_ref[:, sl].astype(F32)
                   + g[:, lane + 2 * NSA_HEADS:lane + 2 * NSA_HEADS + 1] * w_ref[:, sl].astype(F32))
            o_ref[:, sl] = acc.astype(BF16)

    spec = pl.BlockSpec((tr, hw), lambda i: (i, 0))
    return pl.pallas_call(body, out_shape=jax.ShapeDtypeStruct((rows, hw), BF16), grid=(rows // tr,),
                          in_specs=[spec, spec, spec, pl.BlockSpec((tr, LANES), lambda i: (i, 0))],
                          out_specs=spec, compiler_params=_params(("parallel",)), name=name,
                          )(o_cmp, o_sel, o_win, gates)


def _merge(o_fox, o_nsa, w_branch, proj, col_mg, *, tm, tn, tk, name):
    m, kd = o_fox.shape
    d = w_branch.shape[2]
    nk = kd // tk
    assert col_mg % tn == 0 and d % tn == 0

    def body(a0_ref, a1_ref, w_ref, g0_ref, g1_ref, o_ref, acc0, acc1):
        k = pl.program_id(2)
        p0 = _dot(a0_ref[...], w_ref[0])
        p1 = _dot(a1_ref[...], w_ref[1])

        @pl.when(k == 0)
        def _():
            acc0[...] = p0
            acc1[...] = p1

        @pl.when(k > 0)
        def _():
            acc0[...] += p0
            acc1[...] += p1

        @pl.when(k == nk - 1)
        def _():
            o_ref[...] = (_sigmoid(g0_ref[...]) * acc0[...] + _sigmoid(g1_ref[...]) * acc1[...]).astype(BF16)

    a_spec = pl.BlockSpec((tm, tk), lambda i, j, k: (i, k))
    return pl.pallas_call(
        body, out_shape=jax.ShapeDtypeStruct((m, d), BF16), grid=(m // tm, d // tn, nk),
        in_specs=[a_spec, a_spec, pl.BlockSpec((2, tk, tn), lambda i, j, k: (0, k, j)),
                  pl.BlockSpec((tm, tn), lambda i, j, k: (i, col_mg // tn + j)),
                  pl.BlockSpec((tm, tn), lambda i, j, k: (i, (col_mg + d) // tn + j))],
        out_specs=pl.BlockSpec((tm, tn), lambda i, j, k: (i, j)),
        scratch_shapes=[pltpu.VMEM((tm, tn), F32), pltpu.VMEM((tm, tn), F32)],
        compiler_params=_params(("parallel", "parallel", "arbitrary")), name=name,
    )(o_fox, o_nsa, w_branch, proj, proj)


def _route(logits_t, router_bias, *, tn, name):
    n_exp, n_tok = logits_t.shape
    per = n_exp // N_EXPERT_GROUPS
    assert per == SUBLANES

    def body(l_ref, b_ref, e_ref, w_ref):
        scores = _sigmoid(l_ref[...])
        biased = scores + b_ref[...]
        b3 = biased.reshape(N_EXPERT_GROUPS, per, tn)
        sub = lax.broadcasted_iota(jnp.int32, b3.shape, 1)
        m1 = jnp.max(b3, axis=1, keepdims=True)
        i1 = jnp.min(jnp.where(b3 == m1, sub, per), axis=1, keepdims=True)
        m2 = jnp.max(jnp.where(sub == i1, LOWEST, b3), axis=1, keepdims=True)
        gscore = (m1 + m2).reshape(N_EXPERT_GROUPS, tn)
        gid = lax.broadcasted_iota(jnp.int32, gscore.shape, 0)
        gkeep = jnp.zeros(gscore.shape, jnp.bool_)
        for _ in range(TOPK_GROUPS):
            best = jnp.max(gscore, axis=0, keepdims=True)
            first = jnp.min(jnp.where(gscore == best, gid, N_EXPERT_GROUPS), axis=0, keepdims=True)
            hit = gid == first
            gkeep = gkeep | hit
            gscore = jnp.where(hit, LOWEST, gscore)
        emask = jnp.broadcast_to(gkeep[:, None, :], b3.shape).reshape(n_exp, tn)
        masked = jnp.where(emask, biased, NEG)
        eid = lax.broadcasted_iota(jnp.int32, masked.shape, 0)
        ids, wts = [], []
        for _ in range(TOP_K):
            best = jnp.max(masked, axis=0, keepdims=True)
            first = jnp.min(jnp.where(masked == best, eid, n_exp), axis=0, keepdims=True)
            hit = eid == first
            ids.append(first)
            wts.append(jnp.sum(jnp.where(hit, scores, 0.0), axis=0, keepdims=True))
            masked = jnp.where(hit, LOWEST, masked)
        w = jnp.concatenate(wts, axis=0)
        e_ref[...] = jnp.concatenate(ids, axis=0)
        w_ref[...] = w / jnp.sum(w, axis=0, keepdims=True) * ROUTED_SCALE

    return pl.pallas_call(
        body,
        out_shape=(jax.ShapeDtypeStruct((TOP_K, n_tok), jnp.int32), jax.ShapeDtypeStruct((TOP_K, n_tok), F32)),
        grid=(n_tok // tn,),
        in_specs=[pl.BlockSpec((n_exp, tn), lambda i: (0, i)), pl.BlockSpec((n_exp, 1), lambda i: (0, 0))],
        out_specs=(pl.BlockSpec((TOP_K, tn), lambda i: (0, i)), pl.BlockSpec((TOP_K, tn), lambda i: (0, i))),
        compiler_params=_params(("parallel",)), name=name,
    )(logits_t, router_bias.astype(F32).reshape(n_exp, 1))


def _dispatch_tables(eidx_t, n_exp):
    n_tok = eidx_t.shape[1]
    a_cnt = n_tok * TOP_K
    flat_e = eidx_t.T.reshape(a_cnt)
    order = jnp.argsort(flat_e).astype(jnp.int32)
    counts = jnp.sum(flat_e[None, :] == jnp.arange(n_exp, dtype=jnp.int32)[:, None], axis=1).astype(jnp.int32)
    blocks = (counts + MOE_BLOCK - 1) // MOE_BLOCK
    start = jnp.cumsum(counts) - counts
    sb_cnt = (blocks + SUPER_BLOCKS - 1) // SUPER_BLOCKS
    sb_end = jnp.cumsum(sb_cnt)
    n_super = -(-(-(-a_cnt // MOE_BLOCK) + n_exp) // SUPER_BLOCKS) + n_exp
    sid = jnp.arange(n_super, dtype=jnp.int32)
    used = sid < sb_end[-1]
    e_of = jnp.minimum(jnp.searchsorted(sb_end, sid, side='right'), n_exp - 1).astype(jnp.int32)
    e_of = jnp.where(used, e_of, e_of[jnp.maximum(sb_end[-1] - 1, 0)])
    local = sid - (sb_end - sb_cnt)[e_of]
    nb = jnp.where(used, jnp.clip(blocks[e_of] - local * SUPER_BLOCKS, 0, SUPER_BLOCKS), 0).astype(jnp.int32)
    rows = SUPER_BLOCKS * MOE_BLOCK
    r = jnp.arange(rows, dtype=jnp.int32)[None, :]
    within = local[:, None] * rows + r
    real = used[:, None] & (within < counts[e_of][:, None])
    src = jnp.clip(start[e_of][:, None] + within, 0, a_cnt - 1)
    aid = order[src]
    tok = jnp.where(real, aid // TOP_K, 0).astype(jnp.int32)
    dst = jnp.where(real, (aid % TOP_K) * n_tok + aid // TOP_K, a_cnt + r).astype(jnp.int32)
    return e_of, nb, tok.reshape(-1), dst.reshape(-1), n_super


def _experts(packed, e_of, nb, tok, dst, w_gate, w_up, w_down, n_super, n_assign, *, name):
    n_exp, d, hid = w_gate.shape
    half = d // 2
    rows = SUPER_BLOCKS * MOE_BLOCK
    hc = min(HIDDEN_CHUNK, hid)
    n_hc = hid // hc

    def body(e_ref, nb_ref, tok_hbm, dst_hbm, x_hbm, wg_ref, wu_ref, wd_ref, y_hbm,
             tok_s, dst_s, x_scr, y_scr, wg_b, wu_b, wd_b, sem_i, sem_g, sem_s):
        s = pl.program_id(0)
        h = pl.program_id(1)
        n_blk = nb_ref[s]
        n_rows = n_blk * MOE_BLOCK

        def gather_copy(i):
            return pltpu.make_async_copy(x_hbm.at[pl.ds(tok_s[i], 1)], x_scr.at[pl.ds(i, 1)], sem_g)

        def scatter_copy(i):
            return pltpu.make_async_copy(y_scr.at[pl.ds(i, 1)], y_hbm.at[pl.ds(dst_s[i], 1)], sem_s)

        @pl.when((s == 0) & (h == 0))
        def _():
            y_scr[...] = jnp.zeros(y_scr.shape, F32)
            cp = pltpu.make_async_copy(y_scr, y_hbm.at[pl.ds(n_assign, rows)], sem_s)
            cp.start()
            cp.wait()

        @pl.when((n_blk > 0) & (h == 0))
        def _():
            base = pl.multiple_of(s * rows, rows)
            c1 = pltpu.make_async_copy(tok_hbm.at[pl.ds(base, rows)], tok_s, sem_i.at[0])
            c2 = pltpu.make_async_copy(dst_hbm.at[pl.ds(base, rows)], dst_s, sem_i.at[1])
            c1.start()
            c2.start()
            c1.wait()
            c2.wait()

            def issue(i, carry):
                for u in range(DMA_UNROLL):
                    gather_copy(i * DMA_UNROLL + u).start()
                return carry
            lax.fori_loop(0, n_rows // DMA_UNROLL, issue, 0)

            def drain(i, carry):
                for u in range(DMA_UNROLL):
                    gather_copy(i * DMA_UNROLL + u).wait()
                return carry
            lax.fori_loop(0, n_rows // DMA_UNROLL, drain, 0)

        @pl.when(n_blk > 0)
        def _():
            wg_b[...] = wg_ref[0].astype(BF16)
            wu_b[...] = wu_ref[0].astype(BF16)
            wd_b[...] = wd_ref[0].astype(BF16)

            def block(j, carry):
                r0 = pl.multiple_of(j * MOE_BLOCK, MOE_BLOCK)
                u = x_scr[pl.ds(r0, MOE_BLOCK), :]
                x_lo = lax.bitcast_convert_type(u << 16, F32).astype(BF16)
                x_hi = lax.bitcast_convert_type(u & jnp.uint32(0xFFFF0000), F32).astype(BF16)
                gate = (jnp.dot(x_lo, wg_b[:half], preferred_element_type=F32)
                        + jnp.dot(x_hi, wg_b[half:], preferred_element_type=F32))
                up = (jnp.dot(x_lo, wu_b[:half], preferred_element_type=F32)
                      + jnp.dot(x_hi, wu_b[half:], preferred_element_type=F32))
                part = jnp.dot((_silu(gate) * up).astype(BF16), wd_b[...], preferred_element_type=F32)

                @pl.when(h == 0)
                def _():
                    y_scr[pl.ds(r0, MOE_BLOCK), :] = part

                @pl.when(h > 0)
                def _():
                    y_scr[pl.ds(r0, MOE_BLOCK), :] += part
                return carry
            lax.fori_loop(0, n_blk, block, 0)

        @pl.when((n_blk > 0) & (h == n_hc - 1))
        def _():
            def issue(i, carry):
                for u in range(DMA_UNROLL):
                    scatter_copy(i * DMA_UNROLL + u).start()
                return carry
            lax.fori_loop(0, n_rows // DMA_UNROLL, issue, 0)

            def drain(i, carry):
                for u in range(DMA_UNROLL):
                    scatter_copy(i * DMA_UNROLL + u).wait()
                return carry
            lax.fori_loop(0, n_rows // DMA_UNROLL, drain, 0)

    any_spec = pl.BlockSpec(memory_space=pltpu.MemorySpace.HBM)
    return pl.pallas_call(
        body,
        out_shape=jax.ShapeDtypeStruct((n_assign + rows, d), F32),
        grid_spec=pltpu.PrefetchScalarGridSpec(
            num_scalar_prefetch=2, grid=(n_super, n_hc),
            in_specs=[any_spec, any_spec, any_spec,
                      pl.BlockSpec((1, d, hc), lambda s, h, e, nb: (e[s], 0, h)),
                      pl.BlockSpec((1, d, hc), lambda s, h, e, nb: (e[s], 0, h)),
                      pl.BlockSpec((1, hc, d), lambda s, h, e, nb: (e[s], h, 0))],
            out_specs=any_spec,
            scratch_shapes=[pltpu.SMEM((rows,), jnp.int32), pltpu.SMEM((rows,), jnp.int32),
                            pltpu.VMEM((rows, half), jnp.uint32), pltpu.VMEM((rows, d), F32),
                            pltpu.VMEM((d, hc), BF16), pltpu.VMEM((d, hc), BF16), pltpu.VMEM((hc, d), BF16),
                            pltpu.SemaphoreType.DMA((2,)), pltpu.SemaphoreType.DMA, pltpu.SemaphoreType.DMA]),
        compiler_params=_params(("arbitrary", "arbitrary"), has_side_effects=True,
                                disable_bounds_checks=True),
        name=name,
    )(e_of, nb, tok, dst, packed, w_gate, w_up, w_down)


def _shared_hidden(packed, w_gate, w_up, *, tm, tn, name):
    m, half = packed.shape
    hid = w_gate.shape[1]

    def body(x_ref, wg_ref, wu_ref, o_ref):
        u = x_ref[...]
        x_lo = lax.bitcast_convert_type(u << 16, F32).astype(BF16)
        x_hi = lax.bitcast_convert_type(u & jnp.uint32(0xFFFF0000), F32).astype(BF16)
        gate = _dot(x_lo, wg_ref[:half]) + _dot(x_hi, wg_ref[half:])
        up = _dot(x_lo, wu_ref[:half]) + _dot(x_hi, wu_ref[half:])
        o_ref[...] = (_silu(gate) * up).astype(BF16)

    w_spec = pl.BlockSpec((2 * half, tn), lambda i, j: (0, j))
    return pl.pallas_call(
        body, out_shape=jax.ShapeDtypeStruct((m, hid), BF16), grid=(m // tm, hid // tn),
        in_specs=[pl.BlockSpec((tm, half), lambda i, j: (i, 0)), w_spec, w_spec],
        out_specs=pl.BlockSpec((tm, tn), lambda i, j: (i, j)),
        compiler_params=_params(("parallel", "parallel")), name=name,
    )(packed, w_gate, w_up)


def _rope_tables(pos):
    half = ROPE_DIM // 2
    inv = ROPE_THETA ** (-2.0 * jnp.arange(half, dtype=F32) / ROPE_DIM)
    ang = pos.astype(F32)[:, None] * inv[None, :]
    cos, sin = jnp.cos(ang), jnp.sin(ang)
    n = pos.shape[0]
    rest = HEAD_DIM - ROPE_DIM
    c = jnp.concatenate([cos, cos, jnp.ones((n, rest), F32)], axis=1)
    sa = jnp.concatenate([-sin, jnp.zeros((n, half + rest), F32)], axis=1)
    sb = jnp.concatenate([jnp.zeros((n, half), F32), sin, jnp.zeros((n, rest), F32)], axis=1)
    return c, sa, sb


def _overlap_matrix(nch, nsp):
    c = jnp.arange(nch)[:, None]
    s = jnp.arange(nsp)[None, :]
    c_start, c_end = c * CMP_STRIDE, c * CMP_STRIDE + CMP_LEN - 1
    return ((c_start < s * SEL_LEN + SEL_LEN) & (c_end >= s * SEL_LEN)).astype(F32)


def _decay_operand(f_all, n_groups, repeat):
    b_sz, length, heads = f_all.shape
    ft = (f_all * LOG2E).reshape(b_sz, length, n_groups, heads // n_groups).transpose(0, 2, 3, 1)
    return jnp.repeat(ft, repeat, axis=3) if repeat > 1 else ft


def kernel(x_prompt, x_sample, c_prompt, c_sample, cache_fox_k, cache_fox_v, cache_fox_logf, cache_cmp_k, cache_cmp_v, cache_sel_k, cache_sel_v, state_win_k, state_win_v, page_table, w_ada, b_ada, g_mix, g_ffn, w_in, b_forget, g_q_fox, g_k_fox, g_q_nsa, g_k_cmp, g_k_sel, g_k_win, cmp_pe_k, cmp_pe_v, w_cmp_k1, w_cmp_k2, w_cmp_v1, w_cmp_v2, w_branch, w_out, w_router, router_bias, w_exp_gate, w_exp_up, w_exp_down, w_sh_gate, w_sh_up, w_sh_down):
    hd = HEAD_DIM
    bp, seq, d = x_prompt.shape
    bs, dseq, _ = x_sample.shape
    assert bp == 1
    n_pool, page = cache_fox_k.shape[:2]
    past = page_table.shape[1] * page
    w_len = state_win_k.shape[1]
    gf, gn = FOX_KV_HEADS, NSA_KV_HEADS
    assert gf == gn and FOX_HEADS == NSA_HEADS
    n_p, n_s = bp * seq, bs * dseq
    n_all = n_p + n_s
    kvw = gn * hd
    qw = NSA_HEADS * hd
    page_table = page_table.astype(jnp.int32)

    c_all = jnp.concatenate([c_prompt, c_sample], axis=0)
    n_c = c_all.shape[0]
    n_cp = _round_up(n_c, SUBLANES)
    c_all = jnp.pad(c_all, ((0, n_cp - n_c), (0, 0)))
    mod = _matmul(c_all, w_ada, tm=n_cp, tn=_tile(6 * d, 2048, LANES), tk=_tile(d, 512, LANES),
                  out_dtype=F32, name="ada", prologue=_silu,
                  extras=(b_ada.reshape(1, 6 * d),),
                  extra_specs=(((1, _tile(6 * d, 2048, LANES)), lambda i, j, k: (0, j)),),
                  epilogue=lambda r, b: r + b)
    sh1, sc1, ga1, sh2, sc2, ga2 = [mod[:, i * d:(i + 1) * d] for i in range(6)]
    per_row = lambda v: jnp.repeat(v[bp:bp + bs], dseq, axis=0)

    xp2, xs2 = x_prompt.reshape(n_p, d), x_sample.reshape(n_s, d)
    tt_p = _tile(n_p, 256, 16)
    h_p = _norm_mod(xp2, g_mix, sc1[:1], sh1[:1], tt=tt_p, name="norm1_prompt")
    h_s = _norm_mod(xs2, g_mix, per_row(sc1), per_row(sh1), tt=n_s, name="norm1_sample")
    h_all = jnp.concatenate([h_p, h_s], axis=0)

    fw, nw = FOX_HEADS * hd, NSA_HEADS * hd
    kvf = gf * hd
    sizes = [fw, kvf, kvf, FOX_HEADS, nw, kvw, kvw, kvw, kvw, kvw, kvw, 3 * NSA_HEADS, N_BRANCH * d]
    offs = [0]
    for s_ in sizes:
        offs.append(offs[-1] + s_)
    seg = lambda i: w_in[:, offs[i]:offs[i + 1]]
    order = [0, 4, 1, 2, 5, 6, 7, 8, 9, 10, 12, 3, 11]
    assert FOX_HEADS + 3 * NSA_HEADS <= LANES
    np_cols = sum(sizes[i] for i in order[:-2]) + LANES
    tn_in = _tile(_round_up(np_cols, 1280), 1280, LANES) if np_cols > 1280 else np_cols
    np_pad = _round_up(np_cols, tn_in)
    w_in_p = jnp.concatenate([seg(i) for i in order]
                             + [jnp.zeros((d, np_pad - np_cols + LANES - FOX_HEADS - 3 * NSA_HEADS), w_in.dtype)],
                             axis=1).astype(BF16)
    col = {}
    acc_ = 0
    for i in order[:-2]:
        col[i] = acc_
        acc_ += sizes[i]
    col_gates = acc_
    tm_all = _tile(n_all, 1664, 16)
    proj = _matmul(h_all, w_in_p, tm=tm_all, tn=tn_in, tk=_tile(d, 1024, LANES), out_dtype=F32, name="in_proj")

    pos_all = jnp.concatenate([jnp.arange(seq, dtype=jnp.int32)] * bp
                              + [past + jnp.arange(dseq, dtype=jnp.int32)] * bs)
    tabs = _rope_tables(pos_all)
    tr = _tile(n_all, 320, SUBLANES)
    qf = _head_prep(proj, col[0], FOX_HEADS, g_q_fox, None, tr=tr, name="prep_qf")
    kf = _head_prep(proj, col[1], gf, g_k_fox, None, tr=tr, name="prep_kf")
    qn = _head_prep(proj, col[4], NSA_HEADS, g_q_nsa, tabs, tr=tr, name="prep_qn")
    kc = _head_prep(proj, col[5], gn, None, tabs, tr=tr, name="prep_kc")
    ks = _head_prep(proj, col[7], gn, g_k_sel, tabs, tr=tr, name="prep_ks")
    kw = _head_prep(proj, col[9], gn, g_k_win, tabs, tr=tr, name="prep_kw")
    vf = proj[:, col[2]:col[2] + kvf]
    vc = proj[:, col[6]:col[6] + kvw]
    vs = proj[:, col[8]:col[8] + kvw]
    vw = proj[:, col[10]:col[10] + kvw]
    gates = _gates(proj, col_gates, b_forget, tr=tr, name="gates")
    logf = gates[:, :FOX_HEADS]

    ident = lambda nb, n: jnp.arange(nb * n, dtype=jnp.int32).reshape(nb, n)
    pps = PAGES_PER_STEP

    def cmp_weights(pe, w1, w2):
        hid = w1.shape[1]
        halfk = CMP_STRIDE * hd
        assert CMP_LEN == 2 * CMP_STRIDE
        w1ab = jnp.concatenate([w1[:halfk], w1[halfk:]], axis=1).astype(BF16)
        pe_row = jnp.pad(pe.reshape(1, CMP_LEN * hd), ((0, SUBLANES - 1), (0, 0)))
        cpe = _matmul(pe_row, w1, tm=SUBLANES, tn=hid, tk=_tile(CMP_LEN * hd, 1024, LANES),
                      out_dtype=F32, name="cmp_pe")[:1]
        return w1ab, w1[halfk:], cpe, w2.astype(BF16)

    wk = cmp_weights(cmp_pe_k, w_cmp_k1, w_cmp_k2)
    wv = cmp_weights(cmp_pe_v, w_cmp_v1, w_cmp_v2)
    hid_c = w_cmp_k1.shape[1]

    def compress(pages, table, weights, gain, new_rows, name):
        w1ab, w1b, cpe, w2 = weights
        nb = table.shape[0]
        papb = _compress_first(pages, table, w1ab, gn, name=name + "_a")
        if new_rows is None:
            pb_next = jnp.zeros((nb, 1, gn * hid_c), F32)
        else:
            t_new = new_rows.shape[1]
            xn = jnp.pad(new_rows.reshape(nb, t_new, gn, hd), ((0, 0), (0, CMP_STRIDE - t_new), (0, 0), (0, 0)))
            xn = xn.transpose(0, 2, 1, 3).reshape(nb * gn, CMP_STRIDE * hd)
            rows_p = _round_up(nb * gn, SUBLANES)
            xn = jnp.pad(xn, ((0, rows_p - nb * gn), (0, 0)))
            pb_next = _matmul(xn, w1b, tm=rows_p, tn=hid_c, tk=_tile(CMP_STRIDE * hd, 1024, LANES),
                              out_dtype=F32, name=name + "_new")[:nb * gn].reshape(nb, 1, gn * hid_c)
        return _compress_second(papb, pb_next, cpe, w2, gain, gn, name=name + "_b")

    n_pg_p = seq // page
    tab_p = ident(bp, n_pg_p)
    tq_p = _tile(seq, 256, 16)
    tk_p = _tile(seq, 1024, LANES)
    as_pages = lambda a, rows: a[:n_p].reshape(n_p // rows, rows, a.shape[1])
    q_p = lambda a: a[:n_p].reshape(bp, seq, a.shape[1])

    f_p = _cumsum_logf(logf[:n_p].reshape(n_p // page, page * FOX_HEADS // LANES, LANES), tab_p, None,
                       name="cumsum_prompt")
    o_fox_p = _attention(q_p(qf), as_pages(kf, tk_p), as_pages(vf, tk_p), ident(bp, seq // tk_p),
                         tq=tq_p, n_pages=1, q0=0, pos_base=0, fuse_groups=False,
                         decay=_decay_operand(f_p, gf, 1), name="fox_prompt")
    ck_p = compress(kc[:n_p].reshape(n_pg_p, page * gn, hd), tab_p, wk, g_k_cmp, None, "cmpk_prompt")
    cv_p = compress(vc[:n_p].reshape(n_pg_p, page * gn, hd), tab_p, wv, None, None, "cmpv_prompt")
    nch_p = seq // CMP_STRIDE
    ns_p = -(-seq // SEL_LEN)
    nsp_p = _round_up(ns_p, LANES)
    o_cmp_p, mask_p = _cmp_select(q_p(qn), ck_p, cv_p, _overlap_matrix(nch_p, nsp_p), tq=tq_p, q0=0,
                                  nc_valid=nch_p - CMP_LEN // CMP_STRIDE + 1, n_blocks=ns_p, name="cmp_prompt")
    o_sel_p = _attention(q_p(qn), as_pages(ks, tk_p), as_pages(vs, tk_p), ident(bp, seq // tk_p),
                         tq=tq_p, n_pages=1, q0=0, pos_base=0, fuse_groups=False,
                         block_mask=mask_p, name="sel_prompt")
    wtile = min(WINDOW, seq)
    o_win_p = _attention(q_p(qn), as_pages(kw, wtile), as_pages(vw, wtile), ident(bp, seq // wtile),
                         tq=wtile, n_pages=1, q0=0, pos_base=0, fuse_groups=False,
                         window=WINDOW, band_steps=2, name="win_prompt")

    tq_s = _round_up(dseq, SUBLANES)
    q_s = lambda a: jnp.pad(a[n_p:].reshape(bs, dseq, a.shape[1]), ((0, 0), (0, tq_s - dseq), (0, 0)))
    new_page = lambda a, rows: jnp.pad(a[n_p:].reshape(bs, dseq, a.shape[1]), ((0, 0), (0, rows - dseq), (0, 0)))
    rows_pg = lambda c: c.reshape(c.shape[0], -1, hd)
    new_rows = lambda a: rows_pg(new_page(a, page))
    rep_n = NSA_HEADS // gn
    q_dec = lambda a: q_s(a).reshape(bs, tq_s, gn, rep_n, hd).transpose(0, 2, 3, 1, 4).reshape(bs, gn * rep_n * tq_s, hd)
    o_dec = lambda o: o.reshape(bs, gn, rep_n, tq_s, hd).transpose(0, 3, 1, 2, 4).reshape(bs, tq_s, gn * rep_n * hd)

    lf_new = new_page(logf, pps * page).reshape(bs, pps * page * FOX_HEADS // LANES, LANES)
    f_s = _cumsum_logf(cache_fox_logf.astype(F32).reshape(n_pool, page * FOX_HEADS // LANES, LANES),
                       page_table, lf_new, name="cumsum_sample")
    ft_s = _decay_operand(f_s, gf, gf).reshape(bs, FOX_HEADS, -1)
    o_fox_s = o_dec(_attention_dec(q_dec(qf), rows_pg(cache_fox_k), rows_pg(cache_fox_v), page_table,
                                   (new_rows(kf), new_rows(vf)), tq=tq_s, n_groups=gf, n_pages=pps, q0=past,
                                   pos_base=0, extra_pos=past, decay=ft_s, name="fox_sample"))
    kc_new = kc[n_p:].reshape(bs, dseq, kvw)
    vc_new = vc[n_p:].reshape(bs, dseq, kvw)
    ck_s = compress(cache_cmp_k.reshape(n_pool, page * gn, hd), page_table, wk, g_k_cmp, kc_new, "cmpk_sample")
    cv_s = compress(cache_cmp_v.reshape(n_pool, page * gn, hd), page_table, wv, None, vc_new, "cmpv_sample")
    len_s = past + dseq
    nch_s = past // CMP_STRIDE
    n_chunks_s = max(-(-len_s // CMP_STRIDE), CMP_LEN // CMP_STRIDE)
    assert n_chunks_s == nch_s + 1
    ns_s = -(-len_s // SEL_LEN)
    nsp_s = _round_up(max(ns_s, (past + page) // SEL_LEN), LANES)
    o_cmp_s, mask_s = _cmp_select(q_s(qn), ck_s, cv_s, _overlap_matrix(nch_s, nsp_s), tq=tq_s, q0=past,
                                  nc_valid=n_chunks_s - CMP_LEN // CMP_STRIDE + 1, n_blocks=ns_s,
                                  name="cmp_sample")
    bm_s = mask_s.astype(F32).reshape(bs, tq_s, gn, nsp_s).transpose(0, 2, 1, 3).reshape(bs, gn * tq_s, nsp_s)
    o_sel_s = o_dec(_attention_dec(q_dec(qn), rows_pg(cache_sel_k), rows_pg(cache_sel_v), page_table,
                                   (new_rows(ks), new_rows(vs)), tq=tq_s, n_groups=gn, n_pages=pps, q0=past,
                                   pos_base=0, extra_pos=past, block_mask=bm_s, name="sel_sample"))
    wpg = _tile(w_len, page, SUBLANES)
    n_wpg = w_len // wpg
    o_win_s = o_dec(_attention_dec(q_dec(qn), state_win_k.reshape(bs * n_wpg, wpg * gn, hd),
                                   state_win_v.reshape(bs * n_wpg, wpg * gn, hd), ident(bs, n_wpg),
                                   (new_rows(kw), new_rows(vw)), tq=tq_s, n_groups=gn, n_pages=n_wpg, q0=past,
                                   pos_base=past - w_len, extra_pos=past, window=WINDOW, name="win_sample"))

    unpad = lambda a: a[:, :dseq].reshape(n_s, a.shape[2])
    cat = lambda a_p, a_s: jnp.concatenate([a_p.reshape(n_p, -1), unpad(a_s)], axis=0)
    o_nsa = _nsa_combine(cat(o_cmp_p, o_cmp_s), cat(o_sel_p, o_sel_s), cat(o_win_p, o_win_s), gates,
                         tr=_tile(n_all, 320, 16), name="nsa_combine")
    o_fox = cat(o_fox_p, o_fox_s)
    tn_d = _tile(d, 1024, LANES)
    assert col[12] % tn_d == 0
    merged = _merge(o_fox, o_nsa, w_branch, proj, col[12], tm=_tile(n_all, 640, 16), tn=tn_d,
                    tk=_tile(fw, 512, LANES), name="merge")

    resid = lambda r, x_, g_: x_ + g_ * r
    tm_p = _tile(n_p, 1024, 16)
    x1_p = _matmul(merged, w_out, tm=tm_p, tn=tn_d, tk=_tile(d, 1024, LANES), out_dtype=F32, name="out_prompt",
                   m_rows=n_p, extras=(xp2, ga1[:1]),
                   extra_specs=(((tm_p, tn_d), lambda i, j, k: (i, j)),
                                ((1, tn_d), lambda i, j, k: (0, j))), epilogue=resid)
    x1_s = _matmul(merged, w_out, tm=n_s, tn=tn_d, tk=_tile(d, 1024, LANES), out_dtype=F32, name="out_sample",
                   m_rows=n_s, a_row_off=n_p, extras=(xs2, per_row(ga1)),
                   extra_specs=(((n_s, tn_d), lambda i, j, k: (i, j)),
                                ((n_s, tn_d), lambda i, j, k: (i, j))), epilogue=resid)

    n_exp = w_router.shape[1]
    wr_t = w_router.T.astype(F32)
    pk_p, lg_p = _norm_mod(x1_p, g_ffn, sc2[:1], sh2[:1], tt=tt_p, name="norm2_prompt", router_t=wr_t)
    pk_s, lg_s = _norm_mod(x1_s, g_ffn, per_row(sc2), per_row(sh2), tt=n_s, name="norm2_sample", router_t=wr_t)
    packed = jnp.concatenate([pk_p, pk_s], axis=0)
    logits_t = jnp.concatenate([lg_p, lg_s], axis=1)
    n_padr = _round_up(n_all, LANES)
    eidx_t, gw_t = _route(jnp.pad(logits_t, ((0, 0), (0, n_padr - n_all))), router_bias,
                          tn=_tile(n_padr, 640, LANES), name="route")
    eidx_t, gw_t = eidx_t[:, :n_all], gw_t[:, :n_all]
    e_of, nb, tok, dst, n_super = _dispatch_tables(eidx_t, n_exp)
    n_assign = n_all * TOP_K
    ya = _experts(packed, e_of, nb, tok, dst, w_exp_gate, w_exp_up, w_exp_down, n_super, n_assign, name="experts")
    gw = gw_t.T
    hs = _shared_hidden(packed, w_sh_gate, w_sh_up, tm=_tile(n_all, 640, 16),
                        tn=_tile(w_sh_gate.shape[1], 256, LANES), name="shared_hidden")

    def final(r, x_, g_, w_, *ys):
        routed = ys[0] * w_[:, 0:1]
        for k_ in range(1, TOP_K):
            routed = routed + ys[k_] * w_[:, k_:k_ + 1]
        return x_ + g_ * (r + routed)

    w_sd = w_sh_down.astype(BF16)

    def final_call(m_rows, row_off, tm, x1, gate, gate_rows, name):
        assert n_all % tm == 0 and row_off % tm == 0
        ro, per_k = row_off // tm, n_all // tm
        g_spec = ((tm, tn_d), lambda i, j, k: (i, j)) if gate_rows else ((1, tn_d), lambda i, j, k: (0, j))
        y_specs = tuple(((tm, tn_d), functools.partial(lambda i, j, k, kk: (kk * per_k + ro + i, j), kk=kk))
                        for kk in range(TOP_K))
        return _matmul(hs, w_sd, tm=tm, tn=tn_d, tk=_tile(w_sd.shape[0], 1024, LANES), out_dtype=F32,
                       name=name, m_rows=m_rows, a_row_off=row_off, j_outer=True,
                       extras=(x1, gate, gw) + (ya,) * TOP_K,
                       extra_specs=(((tm, tn_d), lambda i, j, k: (i, j)), g_spec,
                                    ((tm, TOP_K), lambda i, j, k: (i + ro, 0))) + y_specs,
                       epilogue=final)

    tm_f = _tile(n_s, 128, 16)
    assert n_p % tm_f == 0
    y_p = final_call(n_p, 0, tm_f, x1_p, ga2[:1], False, "final_prompt")
    y_s = final_call(n_s, n_p, tm_f, x1_s, per_row(ga2), True, "final_sample")

    heads4 = lambda a, lo, hi, b_, t_, g_: a[lo:hi].reshape(b_, t_, g_, hd)
    wl_p = min(WINDOW, seq)
    prompt_state = (heads4(kf, 0, n_p, bp, seq, gf), heads4(vf, 0, n_p, bp, seq, gf),
                    logf[:n_p].reshape(bp, seq, FOX_HEADS),
                    heads4(kc, 0, n_p, bp, seq, gn), heads4(vc, 0, n_p, bp, seq, gn),
                    heads4(ks, 0, n_p, bp, seq, gn), heads4(vs, 0, n_p, bp, seq, gn),
                    heads4(kw, 0, n_p, bp, seq, gn)[:, -wl_p:], heads4(vw, 0, n_p, bp, seq, gn)[:, -wl_p:])
    s4 = lambda a, g_: heads4(a, n_p, n_all, bs, dseq, g_)
    wl_s = min(WINDOW, w_len + dseq)
    win_cat = lambda st, new: jnp.concatenate([st, new.astype(st.dtype)], axis=1)[:, -wl_s:]
    sample_state = (s4(kf, gf), s4(vf, gf), logf[n_p:].reshape(bs, dseq, FOX_HEADS),
                    s4(kc, gn), s4(vc, gn), s4(ks, gn), s4(vs, gn),
                    win_cat(state_win_k, s4(kw, gn)), win_cat(state_win_v, s4(vw, gn)))
    return (y_p.reshape(bp, seq, d), y_s.reshape(bs, dseq, d)) + prompt_state + sample_state
```

```python
import functools

import jax
import jax.numpy as jnp
from jax import lax
from jax.experimental import pallas as pl
from jax.experimental.pallas import tpu as pltpu

F32 = jnp.float32
BF16 = jnp.bfloat16
HIGHEST = lax.Precision.HIGHEST

HEAD_DIM = 128
FOX_HEADS = 16
FOX_KV_HEADS = 4
NSA_HEADS = 16
NSA_KV_HEADS = 4
N_BRANCH = 2
ROPE_THETA = 500000.0
ROPE_DIM = HEAD_DIM // 4
CMP_LEN = 32
CMP_STRIDE = 16
SEL_LEN = 64
N_SEL = 16
WINDOW = 512
N_EXPERT_GROUPS = 8
TOPK_GROUPS = 4
TOP_K = 8
ROUTED_SCALE = 2.5
MOE_BLOCK = 128
EPS = 1e-6
NEG = -1e30
BIG = 1e30
LOWEST = -3.0e38

LANES = 128
SUBLANES = 8
VMEM_LIMIT = 56 * 1024 * 1024
PAGES_PER_STEP = 8
SUPER_BLOCKS = 10
HIDDEN_CHUNK = 256
DOWN_CHUNK = 512
DMA_UNROLL = 8
LOG2E = 1.4426950408889634


def _tile(n, pref, align):
    t = (min(pref, n) // align) * align
    while t >= align:
        if n % t == 0:
            return t
        t -= align
    return n


def _round_up(n, m):
    return -(-n // m) * m


def _params(sem, vmem=VMEM_LIMIT, **kw):
    return pltpu.CompilerParams(dimension_semantics=sem, vmem_limit_bytes=vmem, **kw)


def _dot(a, b):
    return jnp.dot(a.astype(BF16), b.astype(BF16), preferred_element_type=F32)


def _dot_nt(a, b):
    return lax.dot_general(a.astype(BF16), b.astype(BF16), (((1,), (1,)), ((), ())),
                           preferred_element_type=F32)


def _dot_f32(a, b):
    return jnp.dot(a, b, preferred_element_type=F32, precision=HIGHEST)


def _silu(x):
    return x * (1.0 / (1.0 + jnp.exp(-x)))


def _sigmoid(x):
    return 1.0 / (1.0 + jnp.exp(-x))


def _matmul(a, w, *, tm, tn, tk, out_dtype, name, m_rows=None, a_row_off=0,
            extras=(), extra_specs=(), prologue=None, epilogue=None, j_outer=False):
    m = m_rows if m_rows is not None else a.shape[0]
    k_dim, n = w.shape
    assert m % tm == 0 and n % tn == 0 and k_dim % tk == 0 and a_row_off % tm == 0
    nk = k_dim // tk
    ne = len(extras)
    roff = a_row_off // tm

    def body(a_ref, w_ref, *refs):
        ex, o_ref, acc = refs[:ne], refs[ne], refs[ne + 1]
        k = pl.program_id(2)
        av = a_ref[...]
        if prologue is not None:
            av = prologue(av)
        part = _dot(av, w_ref[...])

        @pl.when(k == 0)
        def _():
            acc[...] = part

        @pl.when(k > 0)
        def _():
            acc[...] += part

        @pl.when(k == nk - 1)
        def _():
            r = acc[...]
            if epilogue is not None:
                r = epilogue(r, *[e[...] for e in ex])
            o_ref[...] = r.astype(out_dtype)

    def spec(shape, fn):
        return pl.BlockSpec(shape, (lambda j, i, k: fn(i, j, k)) if j_outer else fn)

    return pl.pallas_call(
        body,
        out_shape=jax.ShapeDtypeStruct((m, n), out_dtype),
        grid=(n // tn, m // tm, nk) if j_outer else (m // tm, n // tn, nk),
        in_specs=[spec((tm, tk), lambda i, j, k: (i + roff, k)),
                  spec((tk, tn), lambda i, j, k: (k, j))] + [spec(s, f) for s, f in extra_specs],
        out_specs=spec((tm, tn), lambda i, j, k: (i, j)),
        scratch_shapes=[pltpu.VMEM((tm, tn), F32)],
        compiler_params=_params(("parallel", "parallel", "arbitrary")),
        name=name,
    )(a, w, *extras)


def _norm_mod(x, gain, scale, shift, *, tt, name, router_t=None):
    m, d = x.shape
    per_row = scale.shape[0] != 1
    mod_spec = (pl.BlockSpec((tt, d), lambda i: (i, 0)) if per_row
                else pl.BlockSpec((1, d), lambda i: (0, 0)))
    routed = router_t is not None

    def body(x_ref, g_ref, sc_ref, sh_ref, *refs):
        xv = x_ref[...]
        y = xv * lax.rsqrt(jnp.mean(xv * xv, axis=-1, keepdims=True) + EPS) * g_ref[...]
        h = y * (1.0 + sc_ref[...]) + sh_ref[...]
        if not routed:
            refs[0][...] = h.astype(BF16)
            return
        r_ref, p_ref, l_ref = refs
        hb = lax.bitcast_convert_type(h.astype(BF16).astype(F32), jnp.uint32)
        half = d // 2
        p_ref[...] = (hb[:, half:] & jnp.uint32(0xFFFF0000)) | (hb[:, :half] >> 16)
        l_ref[...] = lax.dot_general(r_ref[...], h, (((1,), (1,)), ((), ())),
                                     preferred_element_type=F32, precision=HIGHEST)

    in_specs = [pl.BlockSpec((tt, d), lambda i: (i, 0)), pl.BlockSpec((1, d), lambda i: (0, 0)),
                mod_spec, mod_spec]
    args = [x, gain.reshape(1, d), scale, shift]
    if routed:
        e = router_t.shape[0]
        in_specs.append(pl.BlockSpec((e, d), lambda i: (0, 0)))
        args.append(router_t)
        out_shape = (jax.ShapeDtypeStruct((m, d // 2), jnp.uint32), jax.ShapeDtypeStruct((e, m), F32))
        out_specs = (pl.BlockSpec((tt, d // 2), lambda i: (i, 0)), pl.BlockSpec((e, tt), lambda i: (0, i)))
    else:
        out_shape = jax.ShapeDtypeStruct((m, d), BF16)
        out_specs = pl.BlockSpec((tt, d), lambda i: (i, 0))
    return pl.pallas_call(body, out_shape=out_shape, grid=(m // tt,), in_specs=in_specs,
                          out_specs=out_specs, compiler_params=_params(("parallel",)), name=name)(*args)


def _head_prep(proj, col0, n_heads, gain, rope_tabs, *, tr, name):
    rows = proj.shape[0]
    width = n_heads * HEAD_DIM
    assert col0 % width == 0
    do_norm, do_rope = gain is not None, rope_tabs is not None
    half = ROPE_DIM // 2

    def body(*refs):
        x_ref = refs[0]
        pos = 1
        if do_norm:
            g_ref = refs[pos]
            pos += 1
        if do_rope:
            c_ref, sa_ref, sb_ref = refs[pos:pos + 3]
            pos += 3
        o_ref = refs[pos]
        for h in range(n_heads):
            x = x_ref[:, h * HEAD_DIM:(h + 1) * HEAD_DIM]
            if do_norm:
                x = x * lax.rsqrt(jnp.mean(x * x, axis=-1, keepdims=True) + EPS) * g_ref[...]
            if do_rope:
                x = (x * c_ref[...] + pltpu.roll(x, HEAD_DIM - half, 1) * sa_ref[...]
                     + pltpu.roll(x, half, 1) * sb_ref[...])
            o_ref[:, h * HEAD_DIM:(h + 1) * HEAD_DIM] = x

    in_specs = [pl.BlockSpec((tr, width), lambda i: (i, col0 // width))]
    args = [proj]
    if do_norm:
        in_specs.append(pl.BlockSpec((1, HEAD_DIM), lambda i: (0, 0)))
        args.append(gain.reshape(1, HEAD_DIM))
    if do_rope:
        in_specs += [pl.BlockSpec((tr, HEAD_DIM), lambda i: (i, 0))] * 3
        args += list(rope_tabs)
    return pl.pallas_call(body, out_shape=jax.ShapeDtypeStruct((rows, width), F32),
                          grid=(rows // tr,), in_specs=in_specs,
                          out_specs=pl.BlockSpec((tr, width), lambda i: (i, 0)),
                          compiler_params=_params(("parallel",)), name=name)(*args)


def _gates(proj, col0, b_forget, *, tr, name):
    rows = proj.shape[0]
    bias = jnp.zeros((1, LANES), F32).at[0, :FOX_HEADS].set(b_forget.astype(F32))

    def body(x_ref, b_ref, o_ref):
        x = x_ref[...] + b_ref[...]
        lane = lax.broadcasted_iota(jnp.int32, x.shape, 1)
        logsig = jnp.minimum(x, 0.0) - jnp.log(1.0 + jnp.exp(-jnp.abs(x)))
        o_ref[...] = jnp.where(lane < FOX_HEADS, logsig, _sigmoid(x))

    return pl.pallas_call(body, out_shape=jax.ShapeDtypeStruct((rows, LANES), F32),
                          grid=(rows // tr,),
                          in_specs=[pl.BlockSpec((tr, LANES), lambda i: (i, col0 // LANES)),
                                    pl.BlockSpec((1, LANES), lambda i: (0, 0))],
                          out_specs=pl.BlockSpec((tr, LANES), lambda i: (i, 0)),
                          compiler_params=_params(("parallel",)), name=name)(proj, bias)


def _page_specs(n_pages, block_tail, tail_index, page_of_step):
    def make(j):
        def index(*a):
            *gi, pt = a
            b = gi[0]
            return (pt[b, page_of_step(*gi) + j],) + tuple(tail_index(*gi))
        return pl.BlockSpec((1,) + tuple(block_tail), index)
    return [make(j) for j in range(n_pages)]


def _cumsum_logf(pages, page_table, extra, *, name):
    b_sz, n_pg = page_table.shape
    p = PAGES_PER_STEP
    assert n_pg % p == 0 and LANES % FOX_HEADS == 0
    rp = pages.shape[1]
    rows = p * rp
    nst = n_pg // p
    has_x = extra is not None
    steps = nst + (1 if has_x else 0)

    def body(pt_ref, *refs):
        prefs = refs[:p]
        x_ref = refs[p] if has_x else None
        o_ref, carry = refs[p + has_x], refs[p + has_x + 1]
        c = pl.program_id(1)

        @pl.when(c == 0)
        def _():
            carry[...] = jnp.zeros_like(carry)

        def run(x):
            ri = lax.broadcasted_iota(jnp.int32, (LANES, LANES), 0)
            ci = lax.broadcasted_iota(jnp.int32, (LANES, LANES), 1)
            same = (ri % FOX_HEADS) == (ci % FOX_HEADS)
            within = (same & (ri // FOX_HEADS <= ci // FOX_HEADS)).astype(F32)
            total = _dot_f32(x, same.astype(F32))
            rr = lax.broadcasted_iota(jnp.int32, (rows, rows), 0)
            rc = lax.broadcasted_iota(jnp.int32, (rows, rows), 1)
            before = _dot_f32((rc < rr).astype(F32), total)
            o_ref[0] = _dot_f32(x, within) + before + carry[...]
            carry[...] += jnp.sum(total, axis=0, keepdims=True)

        if has_x:
            @pl.when(c < nst)
            def _():
                run(jnp.concatenate([r[0] for r in prefs], axis=0))

            @pl.when(c == nst)
            def _():
                run(x_ref[0])
        else:
            run(jnp.concatenate([r[0] for r in prefs], axis=0))

    in_specs = _page_specs(p, (rp, LANES), lambda b, c: (0, 0),
                           lambda b, c: jnp.minimum(c, nst - 1) * p)
    args = [pages] * p
    if has_x:
        in_specs.append(pl.BlockSpec((1, rows, LANES), lambda b, c, pt: (b, 0, 0)))
        args.append(extra)
    out = pl.pallas_call(
        body,
        out_shape=jax.ShapeDtypeStruct((b_sz, steps * rows, LANES), F32),
        grid_spec=pltpu.PrefetchScalarGridSpec(
            num_scalar_prefetch=1, grid=(b_sz, steps), in_specs=in_specs,
            out_specs=pl.BlockSpec((1, rows, LANES), lambda b, c, pt: (b, c, 0)),
            scratch_shapes=[pltpu.VMEM((1, LANES), F32)]),
        compiler_params=_params(("parallel", "arbitrary")), name=name,
    )(page_table, *args)
    return out.reshape(b_sz, steps * rows * LANES // FOX_HEADS, FOX_HEADS)


def _compress_first(pages, page_table, w1ab, n_groups, *, name):
    b_sz, n_pg = page_table.shape
    p = PAGES_PER_STEP
    assert n_pg % p == 0
    cpp = pages.shape[1] // (CMP_STRIDE * n_groups)
    n2 = w1ab.shape[1]
    rows = p * cpp
    hop = CMP_STRIDE * n_groups

    def body(pt_ref, *refs):
        prefs, w_ref, o_ref = refs[:p], refs[p], refs[p + 1]

        def piece(g, l):
            return jnp.concatenate([r[0, pl.ds(l * n_groups + g, cpp, stride=hop), :] for r in prefs], axis=0)

        lhs = jnp.concatenate(
            [jnp.concatenate([piece(g, l) for l in range(CMP_STRIDE)], axis=1) for g in range(n_groups)], axis=0)
        res = _dot(lhs, w_ref[...])
        for g in range(n_groups):
            o_ref[0, :, g * n2:(g + 1) * n2] = res[g * rows:(g + 1) * rows]

    in_specs = _page_specs(p, pages.shape[1:], lambda b, c: (0, 0), lambda b, c: c * p)
    in_specs.append(pl.BlockSpec(w1ab.shape, lambda b, c, pt: (0, 0)))
    return pl.pallas_call(
        body,
        out_shape=jax.ShapeDtypeStruct((b_sz, n_pg * cpp, n_groups * n2), F32),
        grid_spec=pltpu.PrefetchScalarGridSpec(
            num_scalar_prefetch=1, grid=(b_sz, n_pg // p), in_specs=in_specs,
            out_specs=pl.BlockSpec((1, rows, n_groups * n2), lambda b, c, pt: (b, c, 0))),
        compiler_params=_params(("parallel", "parallel")), name=name,
    )(page_table, *([pages] * p), w1ab)


def _compress_second(papb, pb_next, cpe, w2, gain, n_groups, *, name):
    b_sz, nch, _ = papb.shape
    hid = w2.shape[0]
    do_norm = gain is not None
    g_arr = (gain if do_norm else jnp.ones((HEAD_DIM,), F32)).reshape(1, HEAD_DIM)

    def body(x_ref, n_ref, c_ref, w_ref, g_ref, o_ref):
        row = lax.broadcasted_iota(jnp.int32, (nch, hid), 0)
        for g in range(n_groups):
            pa = x_ref[0, :, g * 2 * hid: g * 2 * hid + hid]
            pb = x_ref[0, :, g * 2 * hid + hid:(g + 1) * 2 * hid]
            nxt = jnp.where(row == nch - 1, n_ref[0, :, g * hid:(g + 1) * hid],
                            pltpu.roll(pb, nch - 1, 0))
            pre = pa + nxt + c_ref[...]
            act = 0.5 * pre * (1.0 + jnp.tanh(0.7978845608028654 * (pre + 0.044715 * pre * pre * pre)))
            y = _dot(act, w_ref[...])
            if do_norm:
                y = y * lax.rsqrt(jnp.mean(y * y, axis=-1, keepdims=True) + EPS) * g_ref[...]
            o_ref[0, :, g * HEAD_DIM:(g + 1) * HEAD_DIM] = y

    return pl.pallas_call(
        body, out_shape=jax.ShapeDtypeStruct((b_sz, nch, n_groups * HEAD_DIM), F32), grid=(b_sz,),
        in_specs=[pl.BlockSpec((1, nch, papb.shape[2]), lambda b: (b, 0, 0)),
                  pl.BlockSpec((1, 1, n_groups * hid), lambda b: (b, 0, 0)),
                  pl.BlockSpec((1, hid), lambda b: (0, 0)),
                  pl.BlockSpec(w2.shape, lambda b: (0, 0)),
                  pl.BlockSpec((1, HEAD_DIM), lambda b: (0, 0))],
        out_specs=pl.BlockSpec((1, nch, n_groups * HEAD_DIM), lambda b: (b, 0, 0)),
        compiler_params=_params(("parallel",)), name=name,
    )(papb, pb_next, cpe, w2, g_arr)


def _cmp_select(q, ck, cv, overlap, *, tq, q0, nc_valid, n_blocks, name):
    b_sz, t_q, hw = q.shape
    nch = ck.shape[1]
    n_groups = ck.shape[2] // HEAD_DIM
    rep = hw // HEAD_DIM // n_groups
    nsp = overlap.shape[1]
    n_keep = min(N_SEL, n_blocks)
    scale = HEAD_DIM ** -0.5

    def body(q_ref, k_ref, v_ref, ov_ref, o_ref, m_ref):
        qi = pl.program_id(2)
        qv = q_ref[0]
        q4 = jnp.concatenate([qv[:, r * HEAD_DIM:(r + 1) * HEAD_DIM] for r in range(rep)], axis=0)
        s = _dot_nt(q4, k_ref[0]) * scale
        qpos = q0 + qi * tq + lax.broadcasted_iota(jnp.int32, (tq, 1), 0)
        cidx = lax.broadcasted_iota(jnp.int32, (1, nch), 1)
        cmask = (cidx * CMP_STRIDE + (CMP_LEN - 1) <= qpos) & (cidx < nc_valid)
        s3 = jnp.where(cmask[None], s.reshape(rep, tq, nch), NEG)
        mx = jnp.max(s3, axis=-1, keepdims=True)
        e = jnp.where(cmask[None], jnp.exp(s3 - mx), 0.0)
        den = jnp.sum(e, axis=-1, keepdims=True)
        p3 = e / jnp.where(den > 0.0, den, 1.0)
        o = _dot(p3.reshape(rep * tq, nch), v_ref[0])
        for r in range(rep):
            o_ref[0, :, r * HEAD_DIM:(r + 1) * HEAD_DIM] = o[r * tq:(r + 1) * tq].astype(BF16)
        imp = _dot_f32(jnp.sum(p3, axis=0), ov_ref[...])
        blk = lax.broadcasted_iota(jnp.int32, (1, nsp), 1)
        cur = qpos // SEL_LEN
        valid = (blk * SEL_LEN <= qpos) & (blk < n_blocks)
        forced = (blk == 0) | (blk == cur) | (blk == cur - 1)
        score = jnp.where(valid & forced, BIG, jnp.where(valid, imp, NEG))
        keep = jnp.zeros((tq, nsp), jnp.bool_)
        for _ in range(n_keep):
            best = jnp.max(score, axis=-1, keepdims=True)
            first = jnp.min(jnp.where(score == best, blk, nsp), axis=-1, keepdims=True)
            hit = blk == first
            keep = keep | hit
            score = jnp.where(hit, LOWEST, score)
        m_ref[0] = jnp.where(keep & valid, 1.0, 0.0).astype(BF16)

    gw = rep * HEAD_DIM
    return pl.pallas_call(
        body,
        out_shape=(jax.ShapeDtypeStruct((b_sz, t_q, hw), BF16),
                   jax.ShapeDtypeStruct((b_sz, t_q, n_groups * nsp), BF16)),
        grid=(b_sz, n_groups, t_q // tq),
        in_specs=[pl.BlockSpec((1, tq, gw), lambda b, g, i: (b, i, g)),
                  pl.BlockSpec((1, nch, HEAD_DIM), lambda b, g, i: (b, 0, g)),
                  pl.BlockSpec((1, nch, HEAD_DIM), lambda b, g, i: (b, 0, g)),
                  pl.BlockSpec(overlap.shape, lambda b, g, i: (0, 0))],
        out_specs=(pl.BlockSpec((1, tq, gw), lambda b, g, i: (b, i, g)),
                   pl.BlockSpec((1, tq, nsp), lambda b, g, i: (b, i, g))),
        compiler_params=_params(("parallel", "parallel", "parallel")), name=name,
    )(q, ck, cv, overlap)


def _attention(q, ksrc, vsrc, page_table, *, tq, n_pages, q0, pos_base, name, fuse_groups,
               extra=None, extra_pos=0, decay=None, block_mask=None, window=None,
               band_steps=None):
    b_sz, t_q, hw = q.shape
    pg = ksrc.shape[1]
    n_groups = ksrc.shape[2] // HEAD_DIM
    rep = hw // HEAD_DIM // n_groups
    n_pg = page_table.shape[1]
    p = n_pages
    assert n_pg % p == 0 and t_q % tq == 0
    nst = n_pg // p
    tk = p * pg
    banded = band_steps is not None
    if banded:
        assert p == 1 and tq == pg and window == pg and extra is None
    has_x = extra is not None
    main_steps = band_steps if banded else nst
    steps = main_steps + (1 if has_x else 0)
    scale = HEAD_DIM ** -0.5
    has_decay, has_bm = decay is not None, block_mask is not None
    nsp = block_mask.shape[2] // n_groups if has_bm else 0
    groups = range(n_groups) if fuse_groups else (None,)
    ng_blk = n_groups if fuse_groups else 1
    kw_blk = ng_blk * HEAD_DIM
    qw_blk = ng_blk * rep * HEAD_DIM

    def norm(gi):
        if fuse_groups:
            b, i, c = gi
            return b, 0, i, c
        return gi

    def last_step(i):
        return jnp.clip((q0 + (i + 1) * tq - 1 - pos_base) // tk, 0, nst - 1)

    def first_page(*gi):
        b, g, i, c = norm(gi)
        if banded:
            return jnp.maximum(i - (band_steps - 1) + c, 0)
        return jnp.minimum(c, last_step(i)) * p

    def body(pt_ref, *refs):
        pos = 0
        q_ref = refs[pos]; pos += 1
        k_refs = refs[pos:pos + p]; pos += p
        v_refs = refs[pos:pos + p]; pos += p
        if has_x:
            kx_ref, vx_ref = refs[pos:pos + 2]; pos += 2
        if has_decay:
            ft_ref = refs[pos]; pos += 1
        if has_bm:
            bm_ref = refs[pos]; pos += 1
        o_ref, m_scr, l_scr, a_scr = refs[pos:pos + 4]
        i = pl.program_id(1 if fuse_groups else 2)
        c = pl.program_id(2 if fuse_groups else 3)

        @pl.when(c == 0)
        def _():
            m_scr[...] = jnp.full(m_scr.shape, NEG, F32)
            l_scr[...] = jnp.zeros(l_scr.shape, F32)
            a_scr[...] = jnp.zeros(a_scr.shape, F32)

        qpos = q0 + i * tq + lax.broadcasted_iota(jnp.int32, (tq, 1), 0)

        def tile(gi, kt, vt, kpos, ft_t, masked=True):
            go = 0 if gi is None else gi
            n = kt.shape[0]
            qv = q_ref[0]
            q4 = jnp.concatenate([qv[:, (go * rep + r) * HEAD_DIM:(go * rep + r + 1) * HEAD_DIM]
                                  for r in range(rep)], axis=0)
            s3 = (_dot_nt(q4, kt) * (scale * LOG2E)).reshape(rep, tq, n)
            if has_decay:
                s3 = s3 - ft_t[:, None, :]
            if masked:
                vis = kpos <= qpos
                if window is not None:
                    vis = vis & (qpos - kpos < window) & (kpos >= pos_base)
                if has_bm:
                    sel = bm_ref[0][:, go * nsp:(go + 1) * nsp]
                    blk = lax.broadcasted_iota(jnp.int32, (nsp, n), 0)
                    expand = (blk == kpos // SEL_LEN).astype(BF16)
                    vis = vis & (jnp.dot(sel, expand, preferred_element_type=F32) > 0.5)
                s3 = jnp.where(vis[None], s3, NEG)
            m_old = m_scr[go]
            m_new = jnp.maximum(m_old, jnp.max(s3, axis=-1, keepdims=True))
            alpha = jnp.exp2(m_old - m_new)
            pr = jnp.exp2(s3 - jnp.concatenate([m_new] * (n // LANES), axis=-1))
            l_scr[go] = alpha * l_scr[go] + jnp.sum(pr, axis=-1, keepdims=True)
            pv = _dot(pr.reshape(rep * tq, n), vt).reshape(rep, tq, HEAD_DIM)
            a_scr[go] = alpha * a_scr[go] + pv
            m_scr[go] = m_new

        def main(masked=True):
            if banded:
                start = (i - (band_steps - 1) + c) * pg + pos_base
            else:
                start = c * tk + pos_base
            kpos = start + lax.broadcasted_iota(jnp.int32, (1, tk), 1)
            for gi in groups:
                go = 0 if gi is None else gi
                lo, hi = go * HEAD_DIM, (go + 1) * HEAD_DIM
                kt = jnp.concatenate([r[0][:, lo:hi] for r in k_refs], axis=0)
                vt = jnp.concatenate([r[0][:, lo:hi] for r in v_refs], axis=0)
                ft_t = ft_ref[0, go] if has_decay else None
                tile(gi, kt, vt, kpos, ft_t, masked)

        if banded:
            main()
        else:
            needed = c <= last_step(i)
            if has_x:
                needed = needed & (c < nst)
            if has_bm or window is not None:
                pl.when(needed)(main)
            else:
                clear = (c + 1) * tk + pos_base - 1 <= q0 + i * tq
                pl.when(needed & clear)(functools.partial(main, False))
                pl.when(needed & jnp.logical_not(clear))(main)

        if has_x:
            @pl.when(c == nst)
            def _():
                nx = kx_ref.shape[1]
                kpos = extra_pos + lax.broadcasted_iota(jnp.int32, (1, nx), 1)
                for gi in groups:
                    go = 0 if gi is None else gi
                    lo, hi = go * HEAD_DIM, (go + 1) * HEAD_DIM
                    ft_t = ft_ref[0, go][:, :nx] if has_decay else None
                    tile(gi, kx_ref[0][:, lo:hi], vx_ref[0][:, lo:hi], kpos, ft_t)

        @pl.when(c == steps - 1)
        def _():
            for gi in groups:
                go = 0 if gi is None else gi
                out = a_scr[go] / l_scr[go]
                for r in range(rep):
                    col = (go * rep + r) * HEAD_DIM
                    o_ref[0, :, col:col + HEAD_DIM] = out[r].astype(BF16)

    def gsel(gi):
        b, g, i, c = norm(gi)
        return b, g, i, c

    in_specs = [pl.BlockSpec((1, tq, qw_blk), lambda *a: (gsel(a[:-1])[0], gsel(a[:-1])[2], gsel(a[:-1])[1]))]
    args = [q]
    kv_tail = lambda *gi: (0, norm(gi)[1])
    in_specs += _page_specs(p, (pg, kw_blk), kv_tail, first_page)
    in_specs += _page_specs(p, (pg, kw_blk), kv_tail, first_page)
    args += [ksrc] * p + [vsrc] * p
    if has_x:
        pgx = extra[0].shape[1]
        xs = pl.BlockSpec((1, pgx, kw_blk), lambda *a: (norm(a[:-1])[0], 0, norm(a[:-1])[1]))
        in_specs += [xs, xs]
        args += list(extra)
    if has_decay:
        def ft_index(*a):
            b, g, i, c = norm(a[:-1])
            step = jnp.where(c >= nst, nst, jnp.minimum(c, last_step(i))) if has_x else jnp.minimum(c, last_step(i))
            return (b, g, 0, step)
        in_specs.append(pl.BlockSpec((1, ng_blk, rep, tk), ft_index))
        args.append(decay)
    if has_bm:
        in_specs.append(pl.BlockSpec((1, tq, ng_blk * nsp),
                                     lambda *a: (norm(a[:-1])[0], norm(a[:-1])[2], norm(a[:-1])[1])))
        args.append(block_mask)
    grid = (b_sz, t_q // tq, steps) if fuse_groups else (b_sz, n_groups, t_q // tq, steps)
    sem = ("parallel",) * (len(grid) - 1) + ("arbitrary",)
    return pl.pallas_call(
        body,
        out_shape=jax.ShapeDtypeStruct((b_sz, t_q, hw), BF16),
        grid_spec=pltpu.PrefetchScalarGridSpec(
            num_scalar_prefetch=1, grid=grid, in_specs=in_specs,
            out_specs=pl.BlockSpec((1, tq, qw_blk),
                                   lambda *a: (norm(a[:-1])[0], norm(a[:-1])[2], norm(a[:-1])[1])),
            scratch_shapes=[pltpu.VMEM((ng_blk, rep, tq, LANES), F32), pltpu.VMEM((ng_blk, rep, tq, LANES), F32),
                            pltpu.VMEM((ng_blk, rep, tq, HEAD_DIM), F32)]),
        compiler_params=_params(sem), name=name,
    )(page_table, *args)


def _attention_dec(q, ksrc, vsrc, page_table, extra, *, tq, n_groups, n_pages, q0, pos_base, extra_pos,
                   name, decay=None, block_mask=None, window=None):
    b_sz, rows, _ = q.shape
    pgg = ksrc.shape[1]
    p = n_pages
    n_pg = page_table.shape[1]
    assert n_pg % p == 0 and pgg % n_groups == 0 and rows % (n_groups * tq) == 0
    nst = n_pg // p
    tk = p * pgg
    heads = rows // tq
    rep = heads // n_groups
    scale = HEAD_DIM ** -0.5
    has_decay, has_bm = decay is not None, block_mask is not None
    nsp = block_mask.shape[2] if has_bm else 0
    nx = extra[0].shape[1]

    def body(pt_ref, *refs):
        pos = 0
        q_ref = refs[pos]; pos += 1
        k_refs = refs[pos:pos + p]; pos += p
        v_refs = refs[pos:pos + p]; pos += p
        kx_ref, vx_ref = refs[pos:pos + 2]; pos += 2
        if has_decay:
            ft_ref = refs[pos]; pos += 1
        if has_bm:
            bm_ref = refs[pos]; pos += 1
        o_ref, m_scr, l_scr, a_scr = refs[pos:pos + 4]
        c = pl.program_id(1)

        @pl.when(c == 0)
        def _():
            m_scr[...] = jnp.full(m_scr.shape, NEG, F32)
            l_scr[...] = jnp.zeros(l_scr.shape, F32)
            a_scr[...] = jnp.zeros(a_scr.shape, F32)

        row = lax.broadcasted_iota(jnp.int32, (rows, 1), 0)
        g_row = row // (rep * tq)
        qpos = q0 + row % tq

        def tile(kt, vt, start, ft_t):
            n = kt.shape[0]
            col = lax.broadcasted_iota(jnp.int32, (1, n), 1)
            kpos = start + col // n_groups
            s = _dot_nt(q_ref[0], kt) * (scale * LOG2E)
            if has_decay:
                s = s - jnp.broadcast_to(ft_t[:, None, :], (heads, tq, n)).reshape(rows, n)
            vis = (col % n_groups == g_row) & (kpos <= qpos)
            if window is not None:
                vis = vis & (qpos - kpos < window)
            if has_bm:
                blk = lax.broadcasted_iota(jnp.int32, (nsp, n), 0)
                expand = (blk == kpos // SEL_LEN).astype(BF16)
                hit = jnp.dot(bm_ref[0].astype(BF16), expand, preferred_element_type=F32)
                hit = jnp.broadcast_to(hit.reshape(n_groups, 1, tq, n), (n_groups, rep, tq, n))
                vis = vis & (hit.reshape(rows, n) > 0.5)
            s = jnp.where(vis, s, NEG)
            m_old = m_scr[...]
            m_new = jnp.maximum(m_old, jnp.max(s, axis=-1, keepdims=True))
            alpha = jnp.exp2(m_old - m_new)
            pr = jnp.exp2(s - jnp.concatenate([m_new] * (n // LANES), axis=-1))
            l_scr[...] = alpha * l_scr[...] + jnp.sum(pr, axis=-1, keepdims=True)
            a_scr[...] = alpha * a_scr[...] + _dot(pr, vt)
            m_scr[...] = m_new

        @pl.when(c < nst)
        def _():
            kt = jnp.concatenate([r[0] for r in k_refs], axis=0)
            vt = jnp.concatenate([r[0] for r in v_refs], axis=0)
            tile(kt, vt, pos_base + c * (tk // n_groups), ft_ref[0] if has_decay else None)

        @pl.when(c == nst)
        def _():
            tile(kx_ref[0], vx_ref[0], extra_pos, ft_ref[0][:, :nx] if has_decay else None)
            o_ref[0] = (a_scr[...] / l_scr[...]).astype(BF16)

    first_page = lambda b, c: jnp.minimum(c, nst - 1) * p
    in_specs = [pl.BlockSpec((1, rows, HEAD_DIM), lambda b, c, pt: (b, 0, 0))]
    in_specs += _page_specs(p, (pgg, HEAD_DIM), lambda b, c: (0, 0), first_page)
    in_specs += _page_specs(p, (pgg, HEAD_DIM), lambda b, c: (0, 0), first_page)
    xs = pl.BlockSpec((1, nx, HEAD_DIM), lambda b, c, pt: (b, 0, 0))
    in_specs += [xs, xs]
    args = [q] + [ksrc] * p + [vsrc] * p + list(extra)
    if has_decay:
        in_specs.append(pl.BlockSpec((1, heads, tk), lambda b, c, pt: (b, 0, c)))
        args.append(decay)
    if has_bm:
        in_specs.append(pl.BlockSpec((1, n_groups * tq, nsp), lambda b, c, pt: (b, 0, 0)))
        args.append(block_mask)
    return pl.pallas_call(
        body,
        out_shape=jax.ShapeDtypeStruct((b_sz, rows, HEAD_DIM), BF16),
        grid_spec=pltpu.PrefetchScalarGridSpec(
            num_scalar_prefetch=1, grid=(b_sz, nst + 1), in_specs=in_specs,
            out_specs=pl.BlockSpec((1, rows, HEAD_DIM), lambda b, c, pt: (b, 0, 0)),
            scratch_shapes=[pltpu.VMEM((rows, LANES), F32), pltpu.VMEM((rows, LANES), F32),
                            pltpu.VMEM((rows, HEAD_DIM), F32)]),
        compiler_params=_params(("parallel", "arbitrary")), name=name,
    )(page_table, *args)


def _nsa_combine(o_cmp, o_sel, o_win, gates, *, tr, name):
    rows, hw = o_cmp.shape

    def body(c_ref, s_ref, w_ref, g_ref, o_ref):
        g = g_ref[...]
        for h in range(NSA_HEADS):
            sl = slice(h * HEAD_DIM, (h + 1) * HEAD_DIM)
            lane = FOX_HEADS + h
            acc = (g[:, lane:lane + 1] * c_ref[:, sl].astype(F32)
                   + g[:, lane + NSA_HEADS:lane + NSA_HEADS + 1] * s_ref[:, sl].astype(F32)
                   + g[:, lane + 2 * NSA_HEADS:lane + 2 * NSA_HEADS + 1] * w_ref[:, sl].astype(F32))
            o_ref[:, sl] = acc.astype(BF16)

    spec = pl.BlockSpec((tr, hw), lambda i: (i, 0))
    return pl.pallas_call(body, out_shape=jax.ShapeDtypeStruct((rows, hw), BF16), grid=(rows // tr,),
                          in_specs=[spec, spec, spec, pl.BlockSpec((tr, LANES), lambda i: (i, 0))],
                          out_specs=spec, compiler_params=_params(("parallel",)), name=name,
                          )(o_cmp, o_sel, o_win, gates)


def _merge(o_fox, o_nsa, w_branch, proj, col_mg, *, tm, tn, tk, name):
    m, kd = o_fox.shape
    d = w_branch.shape[2]
    nk = kd // tk
    assert col_mg % tn == 0 and d % tn == 0

    def body(a0_ref, a1_ref, w_ref, g0_ref, g1_ref, o_ref, acc0, acc1):
        k = pl.program_id(2)
        p0 = _dot(a0_ref[...], w_ref[0])
        p1 = _dot(a1_ref[...], w_ref[1])

        @pl.when(k == 0)
        def _():
            acc0[...] = p0
            acc1[...] = p1

        @pl.when(k > 0)
        def _():
            acc0[...] += p0
            acc1[...] += p1

        @pl.when(k == nk - 1)
        def _():
            o_ref[...] = (_sigmoid(g0_ref[...]) * acc0[...] + _sigmoid(g1_ref[...]) * acc1[...]).astype(BF16)

    a_spec = pl.BlockSpec((tm, tk), lambda i, j, k: (i, k))
    return pl.pallas_call(
        body, out_shape=jax.ShapeDtypeStruct((m, d), BF16), grid=(m // tm, d // tn, nk),
        in_specs=[a_spec, a_spec, pl.BlockSpec((2, tk, tn), lambda i, j, k: (0, k, j)),
                  pl.BlockSpec((tm, tn), lambda i, j, k: (i, col_mg // tn + j)),
                  pl.BlockSpec((tm, tn), lambda i, j, k: (i, (col_mg + d) // tn + j))],
        out_specs=pl.BlockSpec((tm, tn), lambda i, j, k: (i, j)),
        scratch_shapes=[pltpu.VMEM((tm, tn), F32), pltpu.VMEM((tm, tn), F32)],
        compiler_params=_params(("parallel", "parallel", "arbitrary")), name=name,
    )(o_fox, o_nsa, w_branch, proj, proj)


def _route(logits_t, router_bias, *, tn, name):
    n_exp, n_tok = logits_t.shape
    per = n_exp // N_EXPERT_GROUPS
    assert per == SUBLANES

    def body(l_ref, b_ref, e_ref, w_ref):
        scores = _sigmoid(l_ref[...])
        biased = scores + b_ref[...]
        b3 = biased.reshape(N_EXPERT_GROUPS, per, tn)
        sub = lax.broadcasted_iota(jnp.int32, b3.shape, 1)
        m1 = jnp.max(b3, axis=1, keepdims=True)
        i1 = jnp.min(jnp.where(b3 == m1, sub, per), axis=1, keepdims=True)
        m2 = jnp.max(jnp.where(sub == i1, LOWEST, b3), axis=1, keepdims=True)
        gscore = (m1 + m2).reshape(N_EXPERT_GROUPS, tn)
        gid = lax.broadcasted_iota(jnp.int32, gscore.shape, 0)
        gkeep = jnp.zeros(gscore.shape, jnp.bool_)
        for _ in range(TOPK_GROUPS):
            best = jnp.max(gscore, axis=0, keepdims=True)
            first = jnp.min(jnp.where(gscore == best, gid, N_EXPERT_GROUPS), axis=0, keepdims=True)
            hit = gid == first
            gkeep = gkeep | hit
            gscore = jnp.where(hit, LOWEST, gscore)
        emask = jnp.broadcast_to(gkeep[:, None, :], b3.shape).reshape(n_exp, tn)
        masked = jnp.where(emask, biased, NEG)
        eid = lax.broadcasted_iota(jnp.int32, masked.shape, 0)
        ids, wts = [], []
        for _ in range(TOP_K):
            best = jnp.max(masked, axis=0, keepdims=True)
            first = jnp.min(jnp.where(masked == best, eid, n_exp), axis=0, keepdims=True)
            hit = eid == first
            ids.append(first)
            wts.append(jnp.sum(jnp.where(hit, scores, 0.0), axis=0, keepdims=True))
            masked = jnp.where(hit, LOWEST, masked)
        w = jnp.concatenate(wts, axis=0)
        e_ref[...] = jnp.concatenate(ids, axis=0)
        w_ref[...] = w / jnp.sum(w, axis=0, keepdims=True) * ROUTED_SCALE

    return pl.pallas_call(
        body,
        out_shape=(jax.ShapeDtypeStruct((TOP_K, n_tok), jnp.int32), jax.ShapeDtypeStruct((TOP_K, n_tok), F32)),
        grid=(n_tok // tn,),
        in_specs=[pl.BlockSpec((n_exp, tn), lambda i: (0, i)), pl.BlockSpec((n_exp, 1), lambda i: (0, 0))],
        out_specs=(pl.BlockSpec((TOP_K, tn), lambda i: (0, i)), pl.BlockSpec((TOP_K, tn), lambda i: (0, i))),
        compiler_params=_params(("parallel",)), name=name,
    )(logits_t, router_bias.astype(F32).reshape(n_exp, 1))


def _dispatch_tables(eidx_t, n_exp):
    n_tok = eidx_t.shape[1]
    a_cnt = n_tok * TOP_K
    flat_e = eidx_t.T.reshape(a_cnt)
    order = jnp.argsort(flat_e).astype(jnp.int32)
    counts = jnp.sum(flat_e[None, :] == jnp.arange(n_exp, dtype=jnp.int32)[:, None], axis=1).astype(jnp.int32)
    blocks = (counts + MOE_BLOCK - 1) // MOE_BLOCK
    start = jnp.cumsum(counts) - counts
    sb_cnt = (blocks + SUPER_BLOCKS - 1) // SUPER_BLOCKS
    sb_end = jnp.cumsum(sb_cnt)
    n_super = -(-(-(-a_cnt // MOE_BLOCK) + n_exp) // SUPER_BLOCKS) + n_exp
    sid = jnp.arange(n_super, dtype=jnp.int32)
    used = sid < sb_end[-1]
    e_of = jnp.minimum(jnp.searchsorted(sb_end, sid, side='right'), n_exp - 1).astype(jnp.int32)
    e_of = jnp.where(used, e_of, e_of[jnp.maximum(sb_end[-1] - 1, 0)])
    local = sid - (sb_end - sb_cnt)[e_of]
    nb = jnp.where(used, jnp.clip(blocks[e_of] - local * SUPER_BLOCKS, 0, SUPER_BLOCKS), 0).astype(jnp.int32)
    rows = SUPER_BLOCKS * MOE_BLOCK
    r = jnp.arange(rows, dtype=jnp.int32)[None, :]
    within = local[:, None] * rows + r
    real = used[:, None] & (within < counts[e_of][:, None])
    src = jnp.clip(start[e_of][:, None] + within, 0, a_cnt - 1)
    aid = order[src]
    tok = jnp.where(real, aid // TOP_K, 0).astype(jnp.int32)
    dst = jnp.where(real, (aid % TOP_K) * n_tok + aid // TOP_K, a_cnt + r).astype(jnp.int32)
    return e_of, nb, tok.reshape(-1), dst.reshape(-1), n_super


def _experts(packed, e_of, nb, tok, dst, w_gate, w_up, w_down, n_super, n_assign, *, name):
    n_exp, d, hid = w_gate.shape
    half = d // 2
    rows = SUPER_BLOCKS * MOE_BLOCK
    hc = min(HIDDEN_CHUNK, hid)
    n_hc = hid // hc
    dcw = min(DOWN_CHUNK, half)
    n_dc = d // dcw
    n_lo = half // dcw
    steps = n_hc + n_dc
    sub = 2 * MOE_BLOCK

    def unpack(u):
        return (lax.bitcast_convert_type(u << 16, F32).astype(BF16),
                lax.bitcast_convert_type(u & jnp.uint32(0xFFFF0000), F32).astype(BF16))

    def body(e_ref, nb_ref, tok_hbm, dst_hbm, x_hbm, wg_ref, wu_ref, wd_ref, y_hbm,
             tok_s, dst_s, x_scr, h_scr, y_scr, wg_b, wu_b, wd_b, sem_i, sem_g, sem_s):
        s = pl.program_id(0)
        h = pl.program_id(1)
        n_blk = nb_ref[s]
        n_rows = n_blk * MOE_BLOCK

        def for_blocks(fn):
            def pair(j, carry):
                fn(pl.multiple_of(j * sub, sub), sub)
                return carry
            lax.fori_loop(0, n_blk // 2, pair, 0)

            @pl.when(n_blk % 2 == 1)
            def _():
                fn(pl.multiple_of((n_blk - 1) * MOE_BLOCK, MOE_BLOCK), MOE_BLOCK)

        def gather_copy(i):
            return pltpu.make_async_copy(x_hbm.at[pl.ds(tok_s[i], 1)], x_scr.at[pl.ds(i, 1)], sem_g)

        def scatter_copy(i):
            return pltpu.make_async_copy(y_scr.at[pl.ds(i, 1)], y_hbm.at[pl.ds(dst_s[i], 1)], sem_s)

        @pl.when((s == 0) & (h == 0))
        def _():
            y_scr[...] = jnp.zeros(y_scr.shape, jnp.uint32)
            cp = pltpu.make_async_copy(y_scr, y_hbm.at[pl.ds(n_assign, rows)], sem_s)
            cp.start()
            cp.wait()

        @pl.when((n_blk > 0) & (h == 0))
        def _():
            base = pl.multiple_of(s * rows, rows)
            c1 = pltpu.make_async_copy(tok_hbm.at[pl.ds(base, rows)], tok_s, sem_i.at[0])
            c2 = pltpu.make_async_copy(dst_hbm.at[pl.ds(base, rows)], dst_s, sem_i.at[1])
            c1.start()
            c2.start()
            c1.wait()
            c2.wait()

            def issue(i, carry):
                for u in range(DMA_UNROLL):
                    gather_copy(i * DMA_UNROLL + u).start()
                return carry
            lax.fori_loop(0, n_rows // DMA_UNROLL, issue, 0)

            def drain(i, carry):
                for u in range(DMA_UNROLL):
                    gather_copy(i * DMA_UNROLL + u).wait()
                return carry
            lax.fori_loop(0, n_rows // DMA_UNROLL, drain, 0)

        @pl.when((n_blk > 0) & (h < n_hc))
        def _():
            wg_b[...] = wg_ref[0].astype(BF16)
            wu_b[...] = wu_ref[0].astype(BF16)

            def hidden(r0, m):
                x_lo, x_hi = unpack(x_scr[pl.ds(r0, m), :])
                gate = (jnp.dot(x_lo, wg_b[:half], preferred_element_type=F32)
                        + jnp.dot(x_hi, wg_b[half:], preferred_element_type=F32))
                up = (jnp.dot(x_lo, wu_b[:half], preferred_element_type=F32)
                      + jnp.dot(x_hi, wu_b[half:], preferred_element_type=F32))
                h_scr[h, pl.ds(r0, m), :] = (_silu(gate) * up).astype(BF16)
            for_blocks(hidden)

        def down_phase(k):
            col = (k % n_lo) * dcw

            @pl.when((n_blk > 0) & (h == n_hc + k))
            def _():
                wd_b[...] = wd_ref[0].astype(BF16)

                def down(r0, m):
                    y = jnp.dot(h_scr[0, pl.ds(r0, m), :], wd_b[:hc], preferred_element_type=F32)
                    for c in range(1, n_hc):
                        y = y + jnp.dot(h_scr[c, pl.ds(r0, m), :], wd_b[c * hc:(c + 1) * hc],
                                        preferred_element_type=F32)
                    bits = lax.bitcast_convert_type(y.astype(BF16).astype(F32), jnp.uint32)
                    if k < n_lo:
                        y_scr[pl.ds(r0, m), col:col + dcw] = bits >> 16
                    else:
                        y_scr[pl.ds(r0, m), col:col + dcw] = (y_scr[pl.ds(r0, m), col:col + dcw]
                                                              | (bits & jnp.uint32(0xFFFF0000)))
                for_blocks(down)

        for k in range(n_dc):
            down_phase(k)

        @pl.when((n_blk > 0) & (h == steps - 1))
        def _():
            def issue(i, carry):
                for u in range(DMA_UNROLL):
                    scatter_copy(i * DMA_UNROLL + u).start()
                return carry
            lax.fori_loop(0, n_rows // DMA_UNROLL, issue, 0)

            def drain(i, carry):
                for u in range(DMA_UNROLL):
                    scatter_copy(i * DMA_UNROLL + u).wait()
                return carry
            lax.fori_loop(0, n_rows // DMA_UNROLL, drain, 0)

    any_spec = pl.BlockSpec(memory_space=pltpu.MemorySpace.HBM)
    return pl.pallas_call(
        body,
        out_shape=jax.ShapeDtypeStruct((n_assign + rows, half), jnp.uint32),
        grid_spec=pltpu.PrefetchScalarGridSpec(
            num_scalar_prefetch=2, grid=(n_super, steps),
            in_specs=[any_spec, any_spec, any_spec,
                      pl.BlockSpec((1, d, hc), lambda s, h, e, nb: (e[s], 0, jnp.minimum(h, n_hc - 1))),
                      pl.BlockSpec((1, d, hc), lambda s, h, e, nb: (e[s], 0, jnp.minimum(h, n_hc - 1))),
                      pl.BlockSpec((1, hid, dcw), lambda s, h, e, nb: (e[s], 0, jnp.maximum(h - n_hc, 0)))],
            out_specs=any_spec,
            scratch_shapes=[pltpu.SMEM((rows,), jnp.int32), pltpu.SMEM((rows,), jnp.int32),
                            pltpu.VMEM((rows, half), jnp.uint32), pltpu.VMEM((n_hc, rows, hc), BF16),
                            pltpu.VMEM((rows, half), jnp.uint32),
                            pltpu.VMEM((d, hc), BF16), pltpu.VMEM((d, hc), BF16), pltpu.VMEM((hid, dcw), BF16),
                            pltpu.SemaphoreType.DMA((2,)), pltpu.SemaphoreType.DMA, pltpu.SemaphoreType.DMA]),
        compiler_params=_params(("arbitrary", "arbitrary"), has_side_effects=True,
                                disable_bounds_checks=True),
        name=name,
    )(e_of, nb, tok, dst, packed, w_gate, w_up, w_down)


def _shared_hidden(packed, w_gate, w_up, *, tm, tn, name):
    m, half = packed.shape
    hid = w_gate.shape[1]

    def body(x_ref, wg_ref, wu_ref, o_ref):
        u = x_ref[...]
        x_lo = lax.bitcast_convert_type(u << 16, F32).astype(BF16)
        x_hi = lax.bitcast_convert_type(u & jnp.uint32(0xFFFF0000), F32).astype(BF16)
        gate = _dot(x_lo, wg_ref[:half]) + _dot(x_hi, wg_ref[half:])
        up = _dot(x_lo, wu_ref[:half]) + _dot(x_hi, wu_ref[half:])
        o_ref[...] = (_silu(gate) * up).astype(BF16)

    w_spec = pl.BlockSpec((2 * half, tn), lambda i, j: (0, j))
    return pl.pallas_call(
        body, out_shape=jax.ShapeDtypeStruct((m, hid), BF16), grid=(m // tm, hid // tn),
        in_specs=[pl.BlockSpec((tm, half), lambda i, j: (i, 0)), w_spec, w_spec],
        out_specs=pl.BlockSpec((tm, tn), lambda i, j: (i, j)),
        compiler_params=_params(("parallel", "parallel")), name=name,
    )(packed, w_gate, w_up)


def _rope_tables(pos):
    half = ROPE_DIM // 2
    inv = ROPE_THETA ** (-2.0 * jnp.arange(half, dtype=F32) / ROPE_DIM)
    ang = pos.astype(F32)[:, None] * inv[None, :]
    cos, sin = jnp.cos(ang), jnp.sin(ang)
    n = pos.shape[0]
    rest = HEAD_DIM - ROPE_DIM
    c = jnp.concatenate([cos, cos, jnp.ones((n, rest), F32)], axis=1)
    sa = jnp.concatenate([-sin, jnp.zeros((n, half + rest), F32)], axis=1)
    sb = jnp.concatenate([jnp.zeros((n, half), F32), sin, jnp.zeros((n, rest), F32)], axis=1)
    return c, sa, sb


def _overlap_matrix(nch, nsp):
    c = jnp.arange(nch)[:, None]
    s = jnp.arange(nsp)[None, :]
    c_start, c_end = c * CMP_STRIDE, c * CMP_STRIDE + CMP_LEN - 1
    return ((c_start < s * SEL_LEN + SEL_LEN) & (c_end >= s * SEL_LEN)).astype(F32)


def _decay_operand(f_all, n_groups, repeat):
    b_sz, length, heads = f_all.shape
    ft = (f_all * LOG2E).reshape(b_sz, length, n_groups, heads // n_groups).transpose(0, 2, 3, 1)
    return jnp.repeat(ft, repeat, axis=3) if repeat > 1 else ft


def kernel(x_prompt, x_sample, c_prompt, c_sample, cache_fox_k, cache_fox_v, cache_fox_logf, cache_cmp_k, cache_cmp_v, cache_sel_k, cache_sel_v, state_win_k, state_win_v, page_table, w_ada, b_ada, g_mix, g_ffn, w_in, b_forget, g_q_fox, g_k_fox, g_q_nsa, g_k_cmp, g_k_sel, g_k_win, cmp_pe_k, cmp_pe_v, w_cmp_k1, w_cmp_k2, w_cmp_v1, w_cmp_v2, w_branch, w_out, w_router, router_bias, w_exp_gate, w_exp_up, w_exp_down, w_sh_gate, w_sh_up, w_sh_down):
    hd = HEAD_DIM
    bp, seq, d = x_prompt.shape
    bs, dseq, _ = x_sample.shape
    assert bp == 1
    n_pool, page = cache_fox_k.shape[:2]
    past = page_table.shape[1] * page
    w_len = state_win_k.shape[1]
    gf, gn = FOX_KV_HEADS, NSA_KV_HEADS
    assert gf == gn and FOX_HEADS == NSA_HEADS
    n_p, n_s = bp * seq, bs * dseq
    n_all = n_p + n_s
    kvw = gn * hd
    qw = NSA_HEADS * hd
    page_table = page_table.astype(jnp.int32)

    c_all = jnp.concatenate([c_prompt, c_sample], axis=0)
    n_c = c_all.shape[0]
    n_cp = _round_up(n_c, SUBLANES)
    c_all = jnp.pad(c_all, ((0, n_cp - n_c), (0, 0)))
    mod = _matmul(c_all, w_ada, tm=n_cp, tn=_tile(6 * d, 2048, LANES), tk=_tile(d, 512, LANES),
                  out_dtype=F32, name="ada", prologue=_silu,
                  extras=(b_ada.reshape(1, 6 * d),),
                  extra_specs=(((1, _tile(6 * d, 2048, LANES)), lambda i, j, k: (0, j)),),
                  epilogue=lambda r, b: r + b)
    sh1, sc1, ga1, sh2, sc2, ga2 = [mod[:, i * d:(i + 1) * d] for i in range(6)]
    per_row = lambda v: jnp.repeat(v[bp:bp + bs], dseq, axis=0)

    xp2, xs2 = x_prompt.reshape(n_p, d), x_sample.reshape(n_s, d)
    tt_p = _tile(n_p, 256, 16)
    h_p = _norm_mod(xp2, g_mix, sc1[:1], sh1[:1], tt=tt_p, name="norm1_prompt")
    h_s = _norm_mod(xs2, g_mix, per_row(sc1), per_row(sh1), tt=n_s, name="norm1_sample")
    h_all = jnp.concatenate([h_p, h_s], axis=0)

    fw, nw = FOX_HEADS * hd, NSA_HEADS * hd
    kvf = gf * hd
    sizes = [fw, kvf, kvf, FOX_HEADS, nw, kvw, kvw, kvw, kvw, kvw, kvw, 3 * NSA_HEADS, N_BRANCH * d]
    offs = [0]
    for s_ in sizes:
        offs.append(offs[-1] + s_)
    seg = lambda i: w_in[:, offs[i]:offs[i + 1]]
    order = [0, 4, 1, 2, 5, 6, 7, 8, 9, 10, 12, 3, 11]
    assert FOX_HEADS + 3 * NSA_HEADS <= LANES
    np_cols = sum(sizes[i] for i in order[:-2]) + LANES
    tn_in = _tile(_round_up(np_cols, 1280), 1280, LANES) if np_cols > 1280 else np_cols
    np_pad = _round_up(np_cols, tn_in)
    w_in_p = jnp.concatenate([seg(i) for i in order]
                             + [jnp.zeros((d, np_pad - np_cols + LANES - FOX_HEADS - 3 * NSA_HEADS), w_in.dtype)],
                             axis=1).astype(BF16)
    col = {}
    acc_ = 0
    for i in order[:-2]:
        col[i] = acc_
        acc_ += sizes[i]
    col_gates = acc_
    tm_all = _tile(n_all, 1664, 16)
    proj = _matmul(h_all, w_in_p, tm=tm_all, tn=tn_in, tk=_tile(d, 1024, LANES), out_dtype=F32, name="in_proj")

    pos_all = jnp.concatenate([jnp.arange(seq, dtype=jnp.int32)] * bp
                              + [past + jnp.arange(dseq, dtype=jnp.int32)] * bs)
    tabs = _rope_tables(pos_all)
    tr = _tile(n_all, 320, SUBLANES)
    qf = _head_prep(proj, col[0], FOX_HEADS, g_q_fox, None, tr=tr, name="prep_qf")
    kf = _head_prep(proj, col[1], gf, g_k_fox, None, tr=tr, name="prep_kf")
    qn = _head_prep(proj, col[4], NSA_HEADS, g_q_nsa, tabs, tr=tr, name="prep_qn")
    kc = _head_prep(proj, col[5], gn, None, tabs, tr=tr, name="prep_kc")
    ks = _head_prep(proj, col[7], gn, g_k_sel, tabs, tr=tr, name="prep_ks")
    kw = _head_prep(proj, col[9], gn, g_k_win, tabs, tr=tr, name="prep_kw")
    vf = proj[:, col[2]:col[2] + kvf]
    vc = proj[:, col[6]:col[6] + kvw]
    vs = proj[:, col[8]:col[8] + kvw]
    vw = proj[:, col[10]:col[10] + kvw]
    gates = _gates(proj, col_gates, b_forget, tr=tr, name="gates")
    logf = gates[:, :FOX_HEADS]

    ident = lambda nb, n: jnp.arange(nb * n, dtype=jnp.int32).reshape(nb, n)
    pps = PAGES_PER_STEP

    def cmp_weights(pe, w1, w2):
        hid = w1.shape[1]
        halfk = CMP_STRIDE * hd
        assert CMP_LEN == 2 * CMP_STRIDE
        w1ab = jnp.concatenate([w1[:halfk], w1[halfk:]], axis=1).astype(BF16)
        pe_row = jnp.pad(pe.reshape(1, CMP_LEN * hd), ((0, SUBLANES - 1), (0, 0)))
        cpe = _matmul(pe_row, w1, tm=SUBLANES, tn=hid, tk=_tile(CMP_LEN * hd, 1024, LANES),
                      out_dtype=F32, name="cmp_pe")[:1]
        return w1ab, w1[halfk:], cpe, w2.astype(BF16)

    wk = cmp_weights(cmp_pe_k, w_cmp_k1, w_cmp_k2)
    wv = cmp_weights(cmp_pe_v, w_cmp_v1, w_cmp_v2)
    hid_c = w_cmp_k1.shape[1]

    def compress(pages, table, weights, gain, new_rows, name):
        w1ab, w1b, cpe, w2 = weights
        nb = table.shape[0]
        papb = _compress_first(pages, table, w1ab, gn, name=name + "_a")
        if new_rows is None:
            pb_next = jnp.zeros((nb, 1, gn * hid_c), F32)
        else:
            t_new = new_rows.shape[1]
            xn = jnp.pad(new_rows.reshape(nb, t_new, gn, hd), ((0, 0), (0, CMP_STRIDE - t_new), (0, 0), (0, 0)))
            xn = xn.transpose(0, 2, 1, 3).reshape(nb * gn, CMP_STRIDE * hd)
            rows_p = _round_up(nb * gn, SUBLANES)
            xn = jnp.pad(xn, ((0, rows_p - nb * gn), (0, 0)))
            pb_next = _matmul(xn, w1b, tm=rows_p, tn=hid_c, tk=_tile(CMP_STRIDE * hd, 1024, LANES),
                              out_dtype=F32, name=name + "_new")[:nb * gn].reshape(nb, 1, gn * hid_c)
        return _compress_second(papb, pb_next, cpe, w2, gain, gn, name=name + "_b")

    n_pg_p = seq // page
    tab_p = ident(bp, n_pg_p)
    tq_p = _tile(seq, 256, 16)
    tk_p = _tile(seq, 1024, LANES)
    as_pages = lambda a, rows: a[:n_p].reshape(n_p // rows, rows, a.shape[1])
    q_p = lambda a: a[:n_p].reshape(bp, seq, a.shape[1])

    f_p = _cumsum_logf(logf[:n_p].reshape(n_p // page, page * FOX_HEADS // LANES, LANES), tab_p, None,
                       name="cumsum_prompt")
    o_fox_p = _attention(q_p(qf), as_pages(kf, tk_p), as_pages(vf, tk_p), ident(bp, seq // tk_p),
                         tq=tq_p, n_pages=1, q0=0, pos_base=0, fuse_groups=False,
                         decay=_decay_operand(f_p, gf, 1), name="fox_prompt")
    ck_p = compress(kc[:n_p].reshape(n_pg_p, page * gn, hd), tab_p, wk, g_k_cmp, None, "cmpk_prompt")
    cv_p = compress(vc[:n_p].reshape(n_pg_p, page * gn, hd), tab_p, wv, None, None, "cmpv_prompt")
    nch_p = seq // CMP_STRIDE
    ns_p = -(-seq // SEL_LEN)
    nsp_p = _round_up(ns_p, LANES)
    o_cmp_p, mask_p = _cmp_select(q_p(qn), ck_p, cv_p, _overlap_matrix(nch_p, nsp_p), tq=tq_p, q0=0,
                                  nc_valid=nch_p - CMP_LEN // CMP_STRIDE + 1, n_blocks=ns_p, name="cmp_prompt")
    o_sel_p = _attention(q_p(qn), as_pages(ks, tk_p), as_pages(vs, tk_p), ident(bp, seq // tk_p),
                         tq=tq_p, n_pages=1, q0=0, pos_base=0, fuse_groups=False,
                         block_mask=mask_p, name="sel_prompt")
    wtile = min(WINDOW, seq)
    o_win_p = _attention(q_p(qn), as_pages(kw, wtile), as_pages(vw, wtile), ident(bp, seq // wtile),
                         tq=wtile, n_pages=1, q0=0, pos_base=0, fuse_groups=False,
                         window=WINDOW, band_steps=2, name="win_prompt")

    tq_s = _round_up(dseq, SUBLANES)
    q_s = lambda a: jnp.pad(a[n_p:].reshape(bs, dseq, a.shape[1]), ((0, 0), (0, tq_s - dseq), (0, 0)))
    new_page = lambda a, rows: jnp.pad(a[n_p:].reshape(bs, dseq, a.shape[1]), ((0, 0), (0, rows - dseq), (0, 0)))
    rows_pg = lambda c: c.reshape(c.shape[0], -1, hd)
    new_rows = lambda a: rows_pg(new_page(a, page))
    rep_n = NSA_HEADS // gn
    q_dec = lambda a: q_s(a).reshape(bs, tq_s, gn, rep_n, hd).transpose(0, 2, 3, 1, 4).reshape(bs, gn * rep_n * tq_s, hd)
    o_dec = lambda o: o.reshape(bs, gn, rep_n, tq_s, hd).transpose(0, 3, 1, 2, 4).reshape(bs, tq_s, gn * rep_n * hd)

    lf_new = new_page(logf, pps * page).reshape(bs, pps * page * FOX_HEADS // LANES, LANES)
    f_s = _cumsum_logf(cache_fox_logf.astype(F32).reshape(n_pool, page * FOX_HEADS // LANES, LANES),
                       page_table, lf_new, name="cumsum_sample")
    ft_s = _decay_operand(f_s, gf, gf).reshape(bs, FOX_HEADS, -1)
    o_fox_s = o_dec(_attention_dec(q_dec(qf), rows_pg(cache_fox_k), rows_pg(cache_fox_v), page_table,
                                   (new_rows(kf), new_rows(vf)), tq=tq_s, n_groups=gf, n_pages=pps, q0=past,
                                   pos_base=0, extra_pos=past, decay=ft_s, name="fox_sample"))
    kc_new = kc[n_p:].reshape(bs, dseq, kvw)
    vc_new = vc[n_p:].reshape(bs, dseq, kvw)
    ck_s = compress(cache_cmp_k.reshape(n_pool, page * gn, hd), page_table, wk, g_k_cmp, kc_new, "cmpk_sample")
    cv_s = compress(cache_cmp_v.reshape(n_pool, page * gn, hd), page_table, wv, None, vc_new, "cmpv_sample")
    len_s = past + dseq
    nch_s = past // CMP_STRIDE
    n_chunks_s = max(-(-len_s // CMP_STRIDE), CMP_LEN // CMP_STRIDE)
    assert n_chunks_s == nch_s + 1
    ns_s = -(-len_s // SEL_LEN)
    nsp_s = _round_up(max(ns_s, (past + page) // SEL_LEN), LANES)
    o_cmp_s, mask_s = _cmp_select(q_s(qn), ck_s, cv_s, _overlap_matrix(nch_s, nsp_s), tq=tq_s, q0=past,
                                  nc_valid=n_chunks_s - CMP_LEN // CMP_STRIDE + 1, n_blocks=ns_s,
                                  name="cmp_sample")
    bm_s = mask_s.astype(F32).reshape(bs, tq_s, gn, nsp_s).transpose(0, 2, 1, 3).reshape(bs, gn * tq_s, nsp_s)
    o_sel_s = o_dec(_attention_dec(q_dec(qn), rows_pg(cache_sel_k), rows_pg(cache_sel_v), page_table,
                                   (new_rows(ks), new_rows(vs)), tq=tq_s, n_groups=gn, n_pages=pps, q0=past,
                                   pos_base=0, extra_pos=past, block_mask=bm_s, name="sel_sample"))
    wpg = _tile(w_len, page, SUBLANES)
    n_wpg = w_len // wpg
    o_win_s = o_dec(_attention_dec(q_dec(qn), state_win_k.reshape(bs * n_wpg, wpg * gn, hd),
                                   state_win_v.reshape(bs * n_wpg, wpg * gn, hd), ident(bs, n_wpg),
                                   (new_rows(kw), new_rows(vw)), tq=tq_s, n_groups=gn, n_pages=n_wpg, q0=past,
                                   pos_base=past - w_len, extra_pos=past, window=WINDOW, name="win_sample"))

    unpad = lambda a: a[:, :dseq].reshape(n_s, a.shape[2])
    cat = lambda a_p, a_s: jnp.concatenate([a_p.reshape(n_p, -1), unpad(a_s)], axis=0)
    o_nsa = _nsa_combine(cat(o_cmp_p, o_cmp_s), cat(o_sel_p, o_sel_s), cat(o_win_p, o_win_s), gates,
                         tr=_tile(n_all, 320, 16), name="nsa_combine")
    o_fox = cat(o_fox_p, o_fox_s)
    tn_d = _tile(d // 2, 1024, LANES)
    assert col[12] % tn_d == 0
    merged = _merge(o_fox, o_nsa, w_branch, proj, col[12], tm=_tile(n_all, 640, 16), tn=tn_d,
                    tk=_tile(fw, 512, LANES), name="merge")

    resid = lambda r, x_, g_: x_ + g_ * r
    tm_p = _tile(n_p, 1024, 16)
    x1_p = _matmul(merged, w_out, tm=tm_p, tn=tn_d, tk=_tile(d, 1024, LANES), out_dtype=F32, name="out_prompt",
                   m_rows=n_p, extras=(xp2, ga1[:1]),
                   extra_specs=(((tm_p, tn_d), lambda i, j, k: (i, j)),
                                ((1, tn_d), lambda i, j, k: (0, j))), epilogue=resid)
    x1_s = _matmul(merged, w_out, tm=n_s, tn=tn_d, tk=_tile(d, 1024, LANES), out_dtype=F32, name="out_sample",
                   m_rows=n_s, a_row_off=n_p, extras=(xs2, per_row(ga1)),
                   extra_specs=(((n_s, tn_d), lambda i, j, k: (i, j)),
                                ((n_s, tn_d), lambda i, j, k: (i, j))), epilogue=resid)

    n_exp = w_router.shape[1]
    wr_t = w_router.T.astype(F32)
    pk_p, lg_p = _norm_mod(x1_p, g_ffn, sc2[:1], sh2[:1], tt=tt_p, name="norm2_prompt", router_t=wr_t)
    pk_s, lg_s = _norm_mod(x1_s, g_ffn, per_row(sc2), per_row(sh2), tt=n_s, name="norm2_sample", router_t=wr_t)
    packed = jnp.concatenate([pk_p, pk_s], axis=0)
    logits_t = jnp.concatenate([lg_p, lg_s], axis=1)
    n_padr = _round_up(n_all, LANES)
    eidx_t, gw_t = _route(jnp.pad(logits_t, ((0, 0), (0, n_padr - n_all))), router_bias,
                          tn=_tile(n_padr, 640, LANES), name="route")
    eidx_t, gw_t = eidx_t[:, :n_all], gw_t[:, :n_all]
    e_of, nb, tok, dst, n_super = _dispatch_tables(eidx_t, n_exp)
    n_assign = n_all * TOP_K
    ya = _experts(packed, e_of, nb, tok, dst, w_exp_gate, w_exp_up, w_exp_down, n_super, n_assign, name="experts")
    gw = gw_t.T
    hs = _shared_hidden(packed, w_sh_gate, w_sh_up, tm=_tile(n_all, 640, 16),
                        tn=_tile(w_sh_gate.shape[1], 256, LANES), name="shared_hidden")

    n_cb = d // tn_d
    assert n_cb % 2 == 0

    def final(r, x_, g_, w_, *ys):
        low = pl.program_id(0) < n_cb // 2
        val = lambda u: lax.bitcast_convert_type(jnp.where(low, u << 16, u & jnp.uint32(0xFFFF0000)), F32)
        routed = val(ys[0]) * w_[:, 0:1]
        for k_ in range(1, TOP_K):
            routed = routed + val(ys[k_]) * w_[:, k_:k_ + 1]
        return x_ + g_ * (r + routed)

    w_sd = w_sh_down.astype(BF16)

    def final_call(m_rows, row_off, tm, x1, gate, gate_rows, name):
        assert n_all % tm == 0 and row_off % tm == 0
        ro, per_k = row_off // tm, n_all // tm
        g_spec = ((tm, tn_d), lambda i, j, k: (i, j)) if gate_rows else ((1, tn_d), lambda i, j, k: (0, j))
        y_specs = tuple(((tm, tn_d), functools.partial(lambda i, j, k, kk: (kk * per_k + ro + i, j % (n_cb // 2)), kk=kk))
                        for kk in range(TOP_K))
        return _matmul(hs, w_sd, tm=tm, tn=tn_d, tk=_tile(w_sd.shape[0], 1024, LANES), out_dtype=F32,
                       name=name, m_rows=m_rows, a_row_off=row_off, j_outer=True,
                       extras=(x1, gate, gw) + (ya,) * TOP_K,
                       extra_specs=(((tm, tn_d), lambda i, j, k: (i, j)), g_spec,
                                    ((tm, TOP_K), lambda i, j, k: (i + ro, 0))) + y_specs,
                       epilogue=final)

    tm_f = _tile(n_s, 128, 16)
    assert n_p % tm_f == 0
    y_p = final_call(n_p, 0, tm_f, x1_p, ga2[:1], False, "final_prompt")
    y_s = final_call(n_s, n_p, tm_f, x1_s, per_row(ga2), True, "final_sample")

    heads4 = lambda a, lo, hi, b_, t_, g_: a[lo:hi].reshape(b_, t_, g_, hd)
    wl_p = min(WINDOW, seq)
    prompt_state = (heads4(kf, 0, n_p, bp, seq, gf), heads4(vf, 0, n_p, bp, seq, gf),
                    logf[:n_p].reshape(bp, seq, FOX_HEADS),
                    heads4(kc, 0, n_p, bp, seq, gn), heads4(vc, 0, n_p, bp, seq, gn),
                    heads4(ks, 0, n_p, bp, seq, gn), heads4(vs, 0, n_p, bp, seq, gn),
                    heads4(kw, 0, n_p, bp, seq, gn)[:, -wl_p:], heads4(vw, 0, n_p, bp, seq, gn)[:, -wl_p:])
    s4 = lambda a, g_: heads4(a, n_p, n_all, bs, dseq, g_)
    wl_s = min(WINDOW, w_len + dseq)
    win_cat = lambda st, new: jnp.concatenate([st, new.astype(st.dtype)], axis=1)[:, -wl_s:]
    sample_state = (s4(kf, gf), s4(vf, gf), logf[n_p:].reshape(bs, dseq, FOX_HEADS),
                    s4(kc, gn), s4(vc, gn), s4(ks, gn), s4(vs, gn),
                    win_cat(state_win_k, s4(kw, gn)), win_cat(state_win_v, s4(vw, gn)))
    return (y_p.reshape(bp, seq, d), y_s.reshape(bs, dseq, d)) + prompt_state + sample_state
```

```python
import functools

import jax
import jax.numpy as jnp
from jax import lax
from jax.experimental import pallas as pl
from jax.experimental.pallas import tpu as pltpu

F32 = jnp.float32
BF16 = jnp.bfloat16
HIGHEST = lax.Precision.HIGHEST

HEAD_DIM = 128
FOX_HEADS = 16
FOX_KV_HEADS = 4
NSA_HEADS = 16
NSA_KV_HEADS = 4
N_BRANCH = 2
ROPE_THETA = 500000.0
ROPE_DIM = HEAD_DIM // 4
CMP_LEN = 32
CMP_STRIDE = 16
SEL_LEN = 64
N_SEL = 16
WINDOW = 512
N_EXPERT_GROUPS = 8
TOPK_GROUPS = 4
TOP_K = 8
ROUTED_SCALE = 2.5
MOE_BLOCK = 128
EPS = 1e-6
NEG = -1e30
BIG = 1e30
LOWEST = -3.0e38

LANES = 128
SUBLANES = 8
VMEM_LIMIT = 56 * 1024 * 1024
PAGES_PER_STEP = 8
SUPER_BLOCKS = 10
HIDDEN_CHUNK = 256
DOWN_CHUNK = 512
DMA_UNROLL = 8
SMEM_ALIGN = 128
LOG2E = 1.4426950408889634


def _tile(n, pref, align):
    t = (min(pref, n) // align) * align
    while t >= align:
        if n % t == 0:
            return t
        t -= align
    return n


def _round_up(n, m):
    return -(-n // m) * m


def _params(sem, vmem=VMEM_LIMIT, **kw):
    return pltpu.CompilerParams(dimension_semantics=sem, vmem_limit_bytes=vmem, **kw)


def _dot(a, b):
    return jnp.dot(a.astype(BF16), b.astype(BF16), preferred_element_type=F32)


def _dot_nt(a, b):
    return lax.dot_general(a.astype(BF16), b.astype(BF16), (((1,), (1,)), ((), ())),
                           preferred_element_type=F32)


def _dot_f32(a, b):
    return jnp.dot(a, b, preferred_element_type=F32, precision=HIGHEST)


def _silu(x):
    return x * (1.0 / (1.0 + jnp.exp(-x)))


def _sigmoid(x):
    return 1.0 / (1.0 + jnp.exp(-x))


def _matmul(a, w, *, tm, tn, tk, out_dtype, name, m_rows=None, a_row_off=0,
            extras=(), extra_specs=(), prologue=None, epilogue=None, j_outer=False):
    m = m_rows if m_rows is not None else a.shape[0]
    k_dim, n = w.shape
    assert m % tm == 0 and n % tn == 0 and k_dim % tk == 0 and a_row_off % tm == 0
    nk = k_dim // tk
    ne = len(extras)
    roff = a_row_off // tm

    def body(a_ref, w_ref, *refs):
        ex, o_ref, acc = refs[:ne], refs[ne], refs[ne + 1]
        k = pl.program_id(2)
        av = a_ref[...]
        if prologue is not None:
            av = prologue(av)
        part = _dot(av, w_ref[...])

        @pl.when(k == 0)
        def _():
            acc[...] = part

        @pl.when(k > 0)
        def _():
            acc[...] += part

        @pl.when(k == nk - 1)
        def _():
            r = acc[...]
            if epilogue is not None:
                r = epilogue(r, *[e[...] for e in ex])
            o_ref[...] = r.astype(out_dtype)

    def spec(shape, fn):
        return pl.BlockSpec(shape, (lambda j, i, k: fn(i, j, k)) if j_outer else fn)

    return pl.pallas_call(
        body,
        out_shape=jax.ShapeDtypeStruct((m, n), out_dtype),
        grid=(n // tn, m // tm, nk) if j_outer else (m // tm, n // tn, nk),
        in_specs=[spec((tm, tk), lambda i, j, k: (i + roff, k)),
                  spec((tk, tn), lambda i, j, k: (k, j))] + [spec(s, f) for s, f in extra_specs],
        out_specs=spec((tm, tn), lambda i, j, k: (i, j)),
        scratch_shapes=[pltpu.VMEM((tm, tn), F32)],
        compiler_params=_params(("parallel", "parallel", "arbitrary")),
        name=name,
    )(a, w, *extras)


def _norm_mod(x, gain, scale, shift, *, tt, name, router_t=None):
    m, d = x.shape
    per_row = scale.shape[0] != 1
    mod_spec = (pl.BlockSpec((tt, d), lambda i: (i, 0)) if per_row
                else pl.BlockSpec((1, d), lambda i: (0, 0)))
    routed = router_t is not None

    def body(x_ref, g_ref, sc_ref, sh_ref, *refs):
        xv = x_ref[...]
        y = xv * lax.rsqrt(jnp.mean(xv * xv, axis=-1, keepdims=True) + EPS) * g_ref[...]
        h = y * (1.0 + sc_ref[...]) + sh_ref[...]
        if not routed:
            refs[0][...] = h.astype(BF16)
            return
        r_ref, p_ref, l_ref = refs
        hb = lax.bitcast_convert_type(h.astype(BF16).astype(F32), jnp.uint32)
        half = d // 2
        p_ref[...] = (hb[:, half:] & jnp.uint32(0xFFFF0000)) | (hb[:, :half] >> 16)
        l_ref[...] = lax.dot_general(r_ref[...], h, (((1,), (1,)), ((), ())),
                                     preferred_element_type=F32, precision=HIGHEST)

    in_specs = [pl.BlockSpec((tt, d), lambda i: (i, 0)), pl.BlockSpec((1, d), lambda i: (0, 0)),
                mod_spec, mod_spec]
    args = [x, gain.reshape(1, d), scale, shift]
    if routed:
        e = router_t.shape[0]
        in_specs.append(pl.BlockSpec((e, d), lambda i: (0, 0)))
        args.append(router_t)
        out_shape = (jax.ShapeDtypeStruct((m, d // 2), jnp.uint32), jax.ShapeDtypeStruct((e, m), F32))
        out_specs = (pl.BlockSpec((tt, d // 2), lambda i: (i, 0)), pl.BlockSpec((e, tt), lambda i: (0, i)))
    else:
        out_shape = jax.ShapeDtypeStruct((m, d), BF16)
        out_specs = pl.BlockSpec((tt, d), lambda i: (i, 0))
    return pl.pallas_call(body, out_shape=out_shape, grid=(m // tt,), in_specs=in_specs,
                          out_specs=out_specs, compiler_params=_params(("parallel",)), name=name)(*args)


def _head_prep(proj, col0, n_heads, gain, rope_tabs, *, tr, name):
    rows = proj.shape[0]
    width = n_heads * HEAD_DIM
    assert col0 % width == 0
    do_norm, do_rope = gain is not None, rope_tabs is not None
    half = ROPE_DIM // 2

    def body(*refs):
        x_ref = refs[0]
        pos = 1
        if do_norm:
            g_ref = refs[pos]
            pos += 1
        if do_rope:
            c_ref, sa_ref, sb_ref = refs[pos:pos + 3]
            pos += 3
        o_ref = refs[pos]
        for h in range(n_heads):
            x = x_ref[:, h * HEAD_DIM:(h + 1) * HEAD_DIM]
            if do_norm:
                x = x * lax.rsqrt(jnp.mean(x * x, axis=-1, keepdims=True) + EPS) * g_ref[...]
            if do_rope:
                x = (x * c_ref[...] + pltpu.roll(x, HEAD_DIM - half, 1) * sa_ref[...]
                     + pltpu.roll(x, half, 1) * sb_ref[...])
            o_ref[:, h * HEAD_DIM:(h + 1) * HEAD_DIM] = x

    in_specs = [pl.BlockSpec((tr, width), lambda i: (i, col0 // width))]
    args = [proj]
    if do_norm:
        in_specs.append(pl.BlockSpec((1, HEAD_DIM), lambda i: (0, 0)))
        args.append(gain.reshape(1, HEAD_DIM))
    if do_rope:
        in_specs += [pl.BlockSpec((tr, HEAD_DIM), lambda i: (i, 0))] * 3
        args += list(rope_tabs)
    return pl.pallas_call(body, out_shape=jax.ShapeDtypeStruct((rows, width), F32),
                          grid=(rows // tr,), in_specs=in_specs,
                          out_specs=pl.BlockSpec((tr, width), lambda i: (i, 0)),
                          compiler_params=_params(("parallel",)), name=name)(*args)


def _gates(proj, col0, b_forget, *, tr, name):
    rows = proj.shape[0]
    bias = jnp.zeros((1, LANES), F32).at[0, :FOX_HEADS].set(b_forget.astype(F32))

    def body(x_ref, b_ref, o_ref):
        x = x_ref[...] + b_ref[...]
        lane = lax.broadcasted_iota(jnp.int32, x.shape, 1)
        logsig = jnp.minimum(x, 0.0) - jnp.log(1.0 + jnp.exp(-jnp.abs(x)))
        o_ref[...] = jnp.where(lane < FOX_HEADS, logsig, _sigmoid(x))

    return pl.pallas_call(body, out_shape=jax.ShapeDtypeStruct((rows, LANES), F32),
                          grid=(rows // tr,),
                          in_specs=[pl.BlockSpec((tr, LANES), lambda i: (i, col0 // LANES)),
                                    pl.BlockSpec((1, LANES), lambda i: (0, 0))],
                          out_specs=pl.BlockSpec((tr, LANES), lambda i: (i, 0)),
                          compiler_params=_params(("parallel",)), name=name)(proj, bias)


def _page_specs(n_pages, block_tail, tail_index, page_of_step):
    def make(j):
        def index(*a):
            *gi, pt = a
            b = gi[0]
            return (pt[b, page_of_step(*gi) + j],) + tuple(tail_index(*gi))
        return pl.BlockSpec((1,) + tuple(block_tail), index)
    return [make(j) for j in range(n_pages)]


def _cumsum_logf(pages, page_table, extra, *, name):
    b_sz, n_pg = page_table.shape
    p = PAGES_PER_STEP
    assert n_pg % p == 0 and LANES % FOX_HEADS == 0
    rp = pages.shape[1]
    rows = p * rp
    nst = n_pg // p
    has_x = extra is not None
    steps = nst + (1 if has_x else 0)

    def body(pt_ref, *refs):
        prefs = refs[:p]
        x_ref = refs[p] if has_x else None
        o_ref, carry = refs[p + has_x], refs[p + has_x + 1]
        c = pl.program_id(1)

        @pl.when(c == 0)
        def _():
            carry[...] = jnp.zeros_like(carry)

        def run(x):
            ri = lax.broadcasted_iota(jnp.int32, (LANES, LANES), 0)
            ci = lax.broadcasted_iota(jnp.int32, (LANES, LANES), 1)
            same = (ri % FOX_HEADS) == (ci % FOX_HEADS)
            within = (same & (ri // FOX_HEADS <= ci // FOX_HEADS)).astype(F32)
            total = _dot_f32(x, same.astype(F32))
            rr = lax.broadcasted_iota(jnp.int32, (rows, rows), 0)
            rc = lax.broadcasted_iota(jnp.int32, (rows, rows), 1)
            before = _dot_f32((rc < rr).astype(F32), total)
            o_ref[0] = _dot_f32(x, within) + before + carry[...]
            carry[...] += jnp.sum(total, axis=0, keepdims=True)

        if has_x:
            @pl.when(c < nst)
            def _():
                run(jnp.concatenate([r[0] for r in prefs], axis=0))

            @pl.when(c == nst)
            def _():
                run(x_ref[0])
        else:
            run(jnp.concatenate([r[0] for r in prefs], axis=0))

    in_specs = _page_specs(p, (rp, LANES), lambda b, c: (0, 0),
                           lambda b, c: jnp.minimum(c, nst - 1) * p)
    args = [pages] * p
    if has_x:
        in_specs.append(pl.BlockSpec((1, rows, LANES), lambda b, c, pt: (b, 0, 0)))
        args.append(extra)
    out = pl.pallas_call(
        body,
        out_shape=jax.ShapeDtypeStruct((b_sz, steps * rows, LANES), F32),
        grid_spec=pltpu.PrefetchScalarGridSpec(
            num_scalar_prefetch=1, grid=(b_sz, steps), in_specs=in_specs,
            out_specs=pl.BlockSpec((1, rows, LANES), lambda b, c, pt: (b, c, 0)),
            scratch_shapes=[pltpu.VMEM((1, LANES), F32)]),
        compiler_params=_params(("parallel", "arbitrary")), name=name,
    )(page_table, *args)
    return out.reshape(b_sz, steps * rows * LANES // FOX_HEADS, FOX_HEADS)


def _compress_first(pages, page_table, w1ab, n_groups, *, name):
    b_sz, n_pg = page_table.shape
    p = PAGES_PER_STEP
    assert n_pg % p == 0
    cpp = pages.shape[1] // (CMP_STRIDE * n_groups)
    n2 = w1ab.shape[1]
    rows = p * cpp
    hop = CMP_STRIDE * n_groups

    def body(pt_ref, *refs):
        prefs, w_ref, o_ref = refs[:p], refs[p], refs[p + 1]

        def piece(g, l):
            return jnp.concatenate([r[0, pl.ds(l * n_groups + g, cpp, stride=hop), :] for r in prefs], axis=0)

        lhs = jnp.concatenate(
            [jnp.concatenate([piece(g, l) for l in range(CMP_STRIDE)], axis=1) for g in range(n_groups)], axis=0)
        res = _dot(lhs, w_ref[...])
        for g in range(n_groups):
            o_ref[0, :, g * n2:(g + 1) * n2] = res[g * rows:(g + 1) * rows]

    in_specs = _page_specs(p, pages.shape[1:], lambda b, c: (0, 0), lambda b, c: c * p)
    in_specs.append(pl.BlockSpec(w1ab.shape, lambda b, c, pt: (0, 0)))
    return pl.pallas_call(
        body,
        out_shape=jax.ShapeDtypeStruct((b_sz, n_pg * cpp, n_groups * n2), F32),
        grid_spec=pltpu.PrefetchScalarGridSpec(
            num_scalar_prefetch=1, grid=(b_sz, n_pg // p), in_specs=in_specs,
            out_specs=pl.BlockSpec((1, rows, n_groups * n2), lambda b, c, pt: (b, c, 0))),
        compiler_params=_params(("parallel", "parallel")), name=name,
    )(page_table, *([pages] * p), w1ab)


def _compress_second(papb, pb_next, cpe, w2, gain, n_groups, *, name):
    b_sz, nch, _ = papb.shape
    hid = w2.shape[0]
    do_norm = gain is not None
    g_arr = (gain if do_norm else jnp.ones((HEAD_DIM,), F32)).reshape(1, HEAD_DIM)

    def body(x_ref, n_ref, c_ref, w_ref, g_ref, o_ref):
        row = lax.broadcasted_iota(jnp.int32, (nch, hid), 0)
        for g in range(n_groups):
            pa = x_ref[0, :, g * 2 * hid: g * 2 * hid + hid]
            pb = x_ref[0, :, g * 2 * hid + hid:(g + 1) * 2 * hid]
            nxt = jnp.where(row == nch - 1, n_ref[0, :, g * hid:(g + 1) * hid],
                            pltpu.roll(pb, nch - 1, 0))
            pre = pa + nxt + c_ref[...]
            act = 0.5 * pre * (1.0 + jnp.tanh(0.7978845608028654 * (pre + 0.044715 * pre * pre * pre)))
            y = _dot(act, w_ref[...])
            if do_norm:
                y = y * lax.rsqrt(jnp.mean(y * y, axis=-1, keepdims=True) + EPS) * g_ref[...]
            o_ref[0, :, g * HEAD_DIM:(g + 1) * HEAD_DIM] = y

    return pl.pallas_call(
        body, out_shape=jax.ShapeDtypeStruct((b_sz, nch, n_groups * HEAD_DIM), F32), grid=(b_sz,),
        in_specs=[pl.BlockSpec((1, nch, papb.shape[2]), lambda b: (b, 0, 0)),
                  pl.BlockSpec((1, 1, n_groups * hid), lambda b: (b, 0, 0)),
                  pl.BlockSpec((1, hid), lambda b: (0, 0)),
                  pl.BlockSpec(w2.shape, lambda b: (0, 0)),
                  pl.BlockSpec((1, HEAD_DIM), lambda b: (0, 0))],
        out_specs=pl.BlockSpec((1, nch, n_groups * HEAD_DIM), lambda b: (b, 0, 0)),
        compiler_params=_params(("parallel",)), name=name,
    )(papb, pb_next, cpe, w2, g_arr)


def _cmp_select(q, ck, cv, overlap, *, tq, q0, nc_valid, n_blocks, name):
    b_sz, t_q, hw = q.shape
    nch = ck.shape[1]
    n_groups = ck.shape[2] // HEAD_DIM
    rep = hw // HEAD_DIM // n_groups
    nsp = overlap.shape[1]
    overlap = overlap.astype(BF16)
    n_keep = min(N_SEL, n_blocks)
    scale = HEAD_DIM ** -0.5

    def body(q_ref, k_ref, v_ref, ov_ref, o_ref, m_ref):
        qi = pl.program_id(2)
        qv = q_ref[0]
        q4 = jnp.concatenate([qv[:, r * HEAD_DIM:(r + 1) * HEAD_DIM] for r in range(rep)], axis=0)
        s = _dot_nt(q4, k_ref[0]) * scale
        qpos = q0 + qi * tq + lax.broadcasted_iota(jnp.int32, (tq, 1), 0)
        cidx = lax.broadcasted_iota(jnp.int32, (1, nch), 1)
        cmask = (cidx * CMP_STRIDE + (CMP_LEN - 1) <= qpos) & (cidx < nc_valid)
        s3 = jnp.where(cmask[None], s.reshape(rep, tq, nch), NEG)
        mx = jnp.max(s3, axis=-1, keepdims=True)
        e = jnp.where(cmask[None], jnp.exp(s3 - mx), 0.0)
        den = jnp.sum(e, axis=-1, keepdims=True)
        p3 = e / jnp.where(den > 0.0, den, 1.0)
        o = _dot(p3.reshape(rep * tq, nch), v_ref[0])
        for r in range(rep):
            o_ref[0, :, r * HEAD_DIM:(r + 1) * HEAD_DIM] = o[r * tq:(r + 1) * tq].astype(BF16)
        psum = jnp.sum(p3, axis=0)
        hi = psum.astype(BF16)
        rest = psum - hi.astype(F32)
        mid = rest.astype(BF16)
        low = (rest - mid.astype(F32)).astype(BF16)
        parts = jnp.dot(jnp.concatenate([hi.astype(F32), mid.astype(F32), low.astype(F32)], axis=0).astype(BF16),
                        ov_ref[...], preferred_element_type=F32)
        imp = parts[:tq] + parts[tq:2 * tq] + parts[2 * tq:]
        blk = lax.broadcasted_iota(jnp.int32, (1, nsp), 1)
        cur = qpos // SEL_LEN
        valid = (blk * SEL_LEN <= qpos) & (blk < n_blocks)
        forced = (blk == 0) | (blk == cur) | (blk == cur - 1)
        score = jnp.where(valid & forced, BIG, jnp.where(valid, imp, NEG))
        keep = jnp.zeros((tq, nsp), jnp.bool_)
        for _ in range(n_keep):
            best = jnp.max(score, axis=-1, keepdims=True)
            first = jnp.min(jnp.where(score == best, blk, nsp), axis=-1, keepdims=True)
            hit = blk == first
            keep = keep | hit
            score = jnp.where(hit, LOWEST, score)
        m_ref[0] = jnp.where(keep & valid, 1.0, 0.0).astype(BF16)

    gw = rep * HEAD_DIM
    return pl.pallas_call(
        body,
        out_shape=(jax.ShapeDtypeStruct((b_sz, t_q, hw), BF16),
                   jax.ShapeDtypeStruct((b_sz, t_q, n_groups * nsp), BF16)),
        grid=(b_sz, n_groups, t_q // tq),
        in_specs=[pl.BlockSpec((1, tq, gw), lambda b, g, i: (b, i, g)),
                  pl.BlockSpec((1, nch, HEAD_DIM), lambda b, g, i: (b, 0, g)),
                  pl.BlockSpec((1, nch, HEAD_DIM), lambda b, g, i: (b, 0, g)),
                  pl.BlockSpec(overlap.shape, lambda b, g, i: (0, 0))],
        out_specs=(pl.BlockSpec((1, tq, gw), lambda b, g, i: (b, i, g)),
                   pl.BlockSpec((1, tq, nsp), lambda b, g, i: (b, i, g))),
        compiler_params=_params(("parallel", "parallel", "parallel")), name=name,
    )(q, ck, cv, overlap)


def _attention(q, ksrc, vsrc, page_table, *, tq, n_pages, q0, pos_base, name, fuse_groups,
               extra=None, extra_pos=0, decay=None, block_mask=None, window=None,
               band_steps=None):
    b_sz, t_q, hw = q.shape
    pg = ksrc.shape[1]
    n_groups = ksrc.shape[2] // HEAD_DIM
    rep = hw // HEAD_DIM // n_groups
    n_pg = page_table.shape[1]
    p = n_pages
    assert n_pg % p == 0 and t_q % tq == 0
    nst = n_pg // p
    tk = p * pg
    banded = band_steps is not None
    if banded:
        assert p == 1 and tq == pg and window == pg and extra is None
    has_x = extra is not None
    main_steps = band_steps if banded else nst
    steps = main_steps + (1 if has_x else 0)
    scale = HEAD_DIM ** -0.5
    has_decay, has_bm = decay is not None, block_mask is not None
    nsp = block_mask.shape[2] // n_groups if has_bm else 0
    groups = range(n_groups) if fuse_groups else (None,)
    ng_blk = n_groups if fuse_groups else 1
    kw_blk = ng_blk * HEAD_DIM
    qw_blk = ng_blk * rep * HEAD_DIM

    def norm(gi):
        if fuse_groups:
            b, i, c = gi
            return b, 0, i, c
        return gi

    def last_step(i):
        return jnp.clip((q0 + (i + 1) * tq - 1 - pos_base) // tk, 0, nst - 1)

    def first_page(*gi):
        b, g, i, c = norm(gi)
        if banded:
            return jnp.maximum(i - (band_steps - 1) + c, 0)
        return jnp.minimum(c, last_step(i)) * p

    def body(pt_ref, *refs):
        pos = 0
        q_ref = refs[pos]; pos += 1
        k_refs = refs[pos:pos + p]; pos += p
        v_refs = refs[pos:pos + p]; pos += p
        if has_x:
            kx_ref, vx_ref = refs[pos:pos + 2]; pos += 2
        if has_decay:
            ft_ref = refs[pos]; pos += 1
        if has_bm:
            bm_ref = refs[pos]; pos += 1
        o_ref, m_scr, l_scr, a_scr = refs[pos:pos + 4]
        i = pl.program_id(1 if fuse_groups else 2)
        c = pl.program_id(2 if fuse_groups else 3)

        @pl.when(c == 0)
        def _():
            m_scr[...] = jnp.full(m_scr.shape, NEG, F32)
            l_scr[...] = jnp.zeros(l_scr.shape, F32)
            a_scr[...] = jnp.zeros(a_scr.shape, F32)

        qpos = q0 + i * tq + lax.broadcasted_iota(jnp.int32, (tq, 1), 0)

        def tile(gi, kt, vt, kpos, ft_t, masked=True):
            go = 0 if gi is None else gi
            n = kt.shape[0]
            qv = q_ref[0]
            q4 = jnp.concatenate([qv[:, (go * rep + r) * HEAD_DIM:(go * rep + r + 1) * HEAD_DIM]
                                  for r in range(rep)], axis=0)
            s3 = (_dot_nt(q4, kt) * (scale * LOG2E)).reshape(rep, tq, n)
            if has_decay:
                s3 = s3 - ft_t[:, None, :]
            if masked:
                vis = kpos <= qpos
                if window is not None:
                    vis = vis & (qpos - kpos < window) & (kpos >= pos_base)
                if has_bm:
                    sel = bm_ref[0][:, go * nsp:(go + 1) * nsp]
                    blk = lax.broadcasted_iota(jnp.int32, (nsp, n), 0)
                    expand = (blk == kpos // SEL_LEN).astype(BF16)
                    vis = vis & (jnp.dot(sel, expand, preferred_element_type=F32) > 0.5)
                s3 = jnp.where(vis[None], s3, NEG)
            m_old = m_scr[go]
            m_new = jnp.maximum(m_old, jnp.max(s3, axis=-1, keepdims=True))
            alpha = jnp.exp2(m_old - m_new)
            pr = jnp.exp2(s3 - jnp.concatenate([m_new] * (n // LANES), axis=-1))
            l_scr[go] = alpha * l_scr[go] + jnp.sum(pr, axis=-1, keepdims=True)
            pv = _dot(pr.reshape(rep * tq, n), vt).reshape(rep, tq, HEAD_DIM)
            a_scr[go] = alpha * a_scr[go] + pv
            m_scr[go] = m_new

        def main(masked=True):
            if banded:
                start = (i - (band_steps - 1) + c) * pg + pos_base
            else:
                start = c * tk + pos_base
            kpos = start + lax.broadcasted_iota(jnp.int32, (1, tk), 1)
            for gi in groups:
                go = 0 if gi is None else gi
                lo, hi = go * HEAD_DIM, (go + 1) * HEAD_DIM
                kt = jnp.concatenate([r[0][:, lo:hi] for r in k_refs], axis=0)
                vt = jnp.concatenate([r[0][:, lo:hi] for r in v_refs], axis=0)
                ft_t = ft_ref[0, go] if has_decay else None
                tile(gi, kt, vt, kpos, ft_t, masked)

        if banded:
            main()
        else:
            needed = c <= last_step(i)
            if has_x:
                needed = needed & (c < nst)
            if has_bm or window is not None:
                pl.when(needed)(main)
            else:
                clear = (c + 1) * tk + pos_base - 1 <= q0 + i * tq
                pl.when(needed & clear)(functools.partial(main, False))
                pl.when(needed & jnp.logical_not(clear))(main)

        if has_x:
            @pl.when(c == nst)
            def _():
                nx = kx_ref.shape[1]
                kpos = extra_pos + lax.broadcasted_iota(jnp.int32, (1, nx), 1)
                for gi in groups:
                    go = 0 if gi is None else gi
                    lo, hi = go * HEAD_DIM, (go + 1) * HEAD_DIM
                    ft_t = ft_ref[0, go][:, :nx] if has_decay else None
                    tile(gi, kx_ref[0][:, lo:hi], vx_ref[0][:, lo:hi], kpos, ft_t)

        @pl.when(c == steps - 1)
        def _():
            for gi in groups:
                go = 0 if gi is None else gi
                out = a_scr[go] / l_scr[go]
                for r in range(rep):
                    col = (go * rep + r) * HEAD_DIM
                    o_ref[0, :, col:col + HEAD_DIM] = out[r].astype(BF16)

    def gsel(gi):
        b, g, i, c = norm(gi)
        return b, g, i, c

    in_specs = [pl.BlockSpec((1, tq, qw_blk), lambda *a: (gsel(a[:-1])[0], gsel(a[:-1])[2], gsel(a[:-1])[1]))]
    args = [q]
    kv_tail = lambda *gi: (0, norm(gi)[1])
    in_specs += _page_specs(p, (pg, kw_blk), kv_tail, first_page)
    in_specs += _page_specs(p, (pg, kw_blk), kv_tail, first_page)
    args += [ksrc] * p + [vsrc] * p
    if has_x:
        pgx = extra[0].shape[1]
        xs = pl.BlockSpec((1, pgx, kw_blk), lambda *a: (norm(a[:-1])[0], 0, norm(a[:-1])[1]))
        in_specs += [xs, xs]
        args += list(extra)
    if has_decay:
        def ft_index(*a):
            b, g, i, c = norm(a[:-1])
            step = jnp.where(c >= nst, nst, jnp.minimum(c, last_step(i))) if has_x else jnp.minimum(c, last_step(i))
            return (b, g, 0, step)
        in_specs.append(pl.BlockSpec((1, ng_blk, rep, tk), ft_index))
        args.append(decay)
    if has_bm:
        in_specs.append(pl.BlockSpec((1, tq, ng_blk * nsp),
                                     lambda *a: (norm(a[:-1])[0], norm(a[:-1])[2], norm(a[:-1])[1])))
        args.append(block_mask)
    grid = (b_sz, t_q // tq, steps) if fuse_groups else (b_sz, n_groups, t_q // tq, steps)
    sem = ("parallel",) * (len(grid) - 1) + ("arbitrary",)
    return pl.pallas_call(
        body,
        out_shape=jax.ShapeDtypeStruct((b_sz, t_q, hw), BF16),
        grid_spec=pltpu.PrefetchScalarGridSpec(
            num_scalar_prefetch=1, grid=grid, in_specs=in_specs,
            out_specs=pl.BlockSpec((1, tq, qw_blk),
                                   lambda *a: (norm(a[:-1])[0], norm(a[:-1])[2], norm(a[:-1])[1])),
            scratch_shapes=[pltpu.VMEM((ng_blk, rep, tq, LANES), F32), pltpu.VMEM((ng_blk, rep, tq, LANES), F32),
                            pltpu.VMEM((ng_blk, rep, tq, HEAD_DIM), F32)]),
        compiler_params=_params(sem), name=name,
    )(page_table, *args)


def _attention_dec(q, ksrc, vsrc, page_table, extra, *, tq, n_groups, n_pages, q0, pos_base, extra_pos,
                   name, decay=None, block_mask=None, window=None):
    b_sz, rows, _ = q.shape
    pgg = ksrc.shape[1]
    p = n_pages
    n_pg = page_table.shape[1]
    assert n_pg % p == 0 and pgg % n_groups == 0 and rows % (n_groups * tq) == 0
    nst = n_pg // p
    pg = pgg // n_groups
    tk_pos = p * pg
    heads = rows // tq
    rep = heads // n_groups
    scale = HEAD_DIM ** -0.5
    has_decay, has_bm = decay is not None, block_mask is not None
    nsp = block_mask.shape[2] if has_bm else 0
    nx = extra[0].shape[1]
    nx_pos = nx // n_groups

    def body(pt_ref, *refs):
        pos = 0
        q_ref = refs[pos]; pos += 1
        k_refs = refs[pos:pos + p]; pos += p
        v_refs = refs[pos:pos + p]; pos += p
        kx_ref, vx_ref = refs[pos:pos + 2]; pos += 2
        if has_decay:
            ft_ref = refs[pos]; pos += 1
        if has_bm:
            bm_ref = refs[pos]; pos += 1
        o_ref, m_scr, l_scr, a_scr = refs[pos:pos + 4]
        c = pl.program_id(1)

        @pl.when(c == 0)
        def _():
            m_scr[...] = jnp.full(m_scr.shape, NEG, F32)
            l_scr[...] = jnp.zeros(l_scr.shape, F32)
            a_scr[...] = jnp.zeros(a_scr.shape, F32)

        gr = rep * tq
        qpos = q0 + lax.broadcasted_iota(jnp.int32, (gr, 1), 0) % tq

        def tile(load, n, start, ft_t):
            kpos = start + lax.broadcasted_iota(jnp.int32, (1, n), 1)
            vis = kpos <= qpos
            if window is not None:
                vis = vis & (qpos - kpos < window)
            if has_bm:
                blk = lax.broadcasted_iota(jnp.int32, (nsp, n), 0)
                expand = (blk == kpos // SEL_LEN).astype(BF16)
            for g in range(n_groups):
                rs = slice(g * gr, (g + 1) * gr)
                s = _dot_nt(q_ref[0, rs, :], load(0, g)) * (scale * LOG2E)
                if has_decay:
                    s = s - jnp.broadcast_to(ft_t[g * rep:(g + 1) * rep][:, None, :], (rep, tq, n)).reshape(gr, n)
                vis_g = vis
                if has_bm:
                    hit = jnp.dot(bm_ref[0, g * tq:(g + 1) * tq, :].astype(BF16), expand,
                                  preferred_element_type=F32)
                    vis_g = vis & (jnp.broadcast_to(hit[None], (rep, tq, n)).reshape(gr, n) > 0.5)
                s = jnp.where(vis_g, s, NEG)
                m_old = m_scr[rs]
                m_new = jnp.maximum(m_old, jnp.max(s, axis=-1, keepdims=True))
                alpha = jnp.exp2(m_old - m_new)
                pr = jnp.exp2(s - jnp.concatenate([m_new] * (n // LANES), axis=-1))
                l_scr[rs] = alpha * l_scr[rs] + jnp.sum(pr, axis=-1, keepdims=True)
                a_scr[rs] = alpha * a_scr[rs] + _dot(pr, load(1, g))
                m_scr[rs] = m_new

        @pl.when(c < nst)
        def _():
            def load(which, g):
                refs_ = k_refs if which == 0 else v_refs
                return jnp.concatenate([r[0, pl.ds(g, pg, stride=n_groups), :] for r in refs_], axis=0)
            tile(load, tk_pos, pos_base + c * tk_pos, ft_ref[0] if has_decay else None)

        @pl.when(c == nst)
        def _():
            def load(which, g):
                return (kx_ref if which == 0 else vx_ref)[0, pl.ds(g, nx_pos, stride=n_groups), :]
            tile(load, nx_pos, extra_pos, ft_ref[0][:, :nx_pos] if has_decay else None)
            o_ref[0] = (a_scr[...] / l_scr[...]).astype(BF16)

    first_page = lambda b, c: jnp.minimum(c, nst - 1) * p
    in_specs = [pl.BlockSpec((1, rows, HEAD_DIM), lambda b, c, pt: (b, 0, 0))]
    in_specs += _page_specs(p, (pgg, HEAD_DIM), lambda b, c: (0, 0), first_page)
    in_specs += _page_specs(p, (pgg, HEAD_DIM), lambda b, c: (0, 0), first_page)
    xs = pl.BlockSpec((1, nx, HEAD_DIM), lambda b, c, pt: (b, 0, 0))
    in_specs += [xs, xs]
    args = [q] + [ksrc] * p + [vsrc] * p + list(extra)
    if has_decay:
        in_specs.append(pl.BlockSpec((1, heads, tk_pos), lambda b, c, pt: (b, 0, c)))
        args.append(decay)
    if has_bm:
        in_specs.append(pl.BlockSpec((1, n_groups * tq, nsp), lambda b, c, pt: (b, 0, 0)))
        args.append(block_mask)
    return pl.pallas_call(
        body,
        out_shape=jax.ShapeDtypeStruct((b_sz, rows, HEAD_DIM), BF16),
        grid_spec=pltpu.PrefetchScalarGridSpec(
            num_scalar_prefetch=1, grid=(b_sz, nst + 1), in_specs=in_specs,
            out_specs=pl.BlockSpec((1, rows, HEAD_DIM), lambda b, c, pt: (b, 0, 0)),
            scratch_shapes=[pltpu.VMEM((rows, LANES), F32), pltpu.VMEM((rows, LANES), F32),
                            pltpu.VMEM((rows, HEAD_DIM), F32)]),
        compiler_params=_params(("parallel", "arbitrary")), name=name,
    )(page_table, *args)


def _nsa_combine(o_cmp, o_sel, o_win, gates, *, tr, name):
    rows, hw = o_cmp.shape

    def body(c_ref, s_ref, w_ref, g_ref, o_ref):
        g = g_ref[...]
        for h in range(NSA_HEADS):
            sl = slice(h * HEAD_DIM, (h + 1) * HEAD_DIM)
            lane = FOX_HEADS + h
            acc = (g[:, lane:lane + 1] * c_ref[:, sl].astype(F32)
                   + g[:, lane + NSA_HEADS:lane + NSA_HEADS + 1] * s_ref[:, sl].astype(F32)
                   + g[:, lane + 2 * NSA_HEADS:lane + 2 * NSA_HEADS + 1] * w_ref[:, sl].astype(F32))
            o_ref[:, sl] = acc.astype(BF16)

    spec = pl.BlockSpec((tr, hw), lambda i: (i, 0))
    return pl.pallas_call(body, out_shape=jax.ShapeDtypeStruct((rows, hw), BF16), grid=(rows // tr,),
                          in_specs=[spec, spec, spec, pl.BlockSpec((tr, LANES), lambda i: (i, 0))],
                          out_specs=spec, compiler_params=_params(("parallel",)), name=name,
                          )(o_cmp, o_sel, o_win, gates)


def _merge(o_fox, o_nsa, w_branch, proj, col_mg, *, tm, tn, tk, name):
    m, kd = o_fox.shape
    d = w_branch.shape[2]
    nk = kd // tk
    assert col_mg % tn == 0 and d % tn == 0

    def body(a0_ref, a1_ref, w_ref, g0_ref, g1_ref, o_ref, acc0, acc1):
        k = pl.program_id(2)
        p0 = _dot(a0_ref[...], w_ref[0])
        p1 = _dot(a1_ref[...], w_ref[1])

        @pl.when(k == 0)
        def _():
            acc0[...] = p0
            acc1[...] = p1

        @pl.when(k > 0)
        def _():
            acc0[...] += p0
            acc1[...] += p1

        @pl.when(k == nk - 1)
        def _():
            o_ref[...] = (_sigmoid(g0_ref[...]) * acc0[...] + _sigmoid(g1_ref[...]) * acc1[...]).astype(BF16)

    a_spec = pl.BlockSpec((tm, tk), lambda i, j, k: (i, k))
    return pl.pallas_call(
        body, out_shape=jax.ShapeDtypeStruct((m, d), BF16), grid=(m // tm, d // tn, nk),
        in_specs=[a_spec, a_spec, pl.BlockSpec((2, tk, tn), lambda i, j, k: (0, k, j)),
                  pl.BlockSpec((tm, tn), lambda i, j, k: (i, col_mg // tn + j)),
                  pl.BlockSpec((tm, tn), lambda i, j, k: (i, (col_mg + d) // tn + j))],
        out_specs=pl.BlockSpec((tm, tn), lambda i, j, k: (i, j)),
        scratch_shapes=[pltpu.VMEM((tm, tn), F32), pltpu.VMEM((tm, tn), F32)],
        compiler_params=_params(("parallel", "parallel", "arbitrary")), name=name,
    )(o_fox, o_nsa, w_branch, proj, proj)


def _route(logits_t, router_bias, *, tn, name):
    n_exp, n_tok = logits_t.shape
    per = n_exp // N_EXPERT_GROUPS
    assert per == SUBLANES

    def body(l_ref, b_ref, e_ref, w_ref):
        scores = _sigmoid(l_ref[...])
        biased = scores + b_ref[...]
        b3 = biased.reshape(N_EXPERT_GROUPS, per, tn)
        sub = lax.broadcasted_iota(jnp.int32, b3.shape, 1)
        m1 = jnp.max(b3, axis=1, keepdims=True)
        i1 = jnp.min(jnp.where(b3 == m1, sub, per), axis=1, keepdims=True)
        m2 = jnp.max(jnp.where(sub == i1, LOWEST, b3), axis=1, keepdims=True)
        gscore = (m1 + m2).reshape(N_EXPERT_GROUPS, tn)
        gid = lax.broadcasted_iota(jnp.int32, gscore.shape, 0)
        gkeep = jnp.zeros(gscore.shape, jnp.bool_)
        for _ in range(TOPK_GROUPS):
            best = jnp.max(gscore, axis=0, keepdims=True)
            first = jnp.min(jnp.where(gscore == best, gid, N_EXPERT_GROUPS), axis=0, keepdims=True)
            hit = gid == first
            gkeep = gkeep | hit
            gscore = jnp.where(hit, LOWEST, gscore)
        emask = jnp.broadcast_to(gkeep[:, None, :], b3.shape).reshape(n_exp, tn)
        masked = jnp.where(emask, biased, NEG)
        eid = lax.broadcasted_iota(jnp.int32, masked.shape, 0)
        ids, wts = [], []
        for _ in range(TOP_K):
            best = jnp.max(masked, axis=0, keepdims=True)
            first = jnp.min(jnp.where(masked == best, eid, n_exp), axis=0, keepdims=True)
            hit = eid == first
            ids.append(first)
            wts.append(jnp.sum(jnp.where(hit, scores, 0.0), axis=0, keepdims=True))
            masked = jnp.where(hit, LOWEST, masked)
        w = jnp.concatenate(wts, axis=0)
        e_ref[...] = jnp.concatenate(ids, axis=0)
        w_ref[...] = w / jnp.sum(w, axis=0, keepdims=True) * ROUTED_SCALE

    return pl.pallas_call(
        body,
        out_shape=(jax.ShapeDtypeStruct((TOP_K, n_tok), jnp.int32), jax.ShapeDtypeStruct((TOP_K, n_tok), F32)),
        grid=(n_tok // tn,),
        in_specs=[pl.BlockSpec((n_exp, tn), lambda i: (0, i)), pl.BlockSpec((n_exp, 1), lambda i: (0, 0))],
        out_specs=(pl.BlockSpec((TOP_K, tn), lambda i: (0, i)), pl.BlockSpec((TOP_K, tn), lambda i: (0, i))),
        compiler_params=_params(("parallel",)), name=name,
    )(logits_t, router_bias.astype(F32).reshape(n_exp, 1))


def _dispatch_tables(eidx_t, n_exp):
    n_tok = eidx_t.shape[1]
    a_cnt = n_tok * TOP_K
    flat_e = eidx_t.T.reshape(a_cnt)
    order = jnp.argsort(flat_e).astype(jnp.int32)
    counts = jnp.sum(flat_e[None, :] == jnp.arange(n_exp, dtype=jnp.int32)[:, None], axis=1).astype(jnp.int32)
    blocks = (counts + MOE_BLOCK - 1) // MOE_BLOCK
    start = jnp.cumsum(counts) - counts
    sb_cnt = (blocks + SUPER_BLOCKS - 1) // SUPER_BLOCKS
    sb_end = jnp.cumsum(sb_cnt)
    n_super = -(-(-(-a_cnt // MOE_BLOCK) + n_exp) // SUPER_BLOCKS) + n_exp
    sid = jnp.arange(n_super, dtype=jnp.int32)
    used = sid < sb_end[-1]
    e_of = jnp.minimum(jnp.searchsorted(sb_end, sid, side='right'), n_exp - 1).astype(jnp.int32)
    e_of = jnp.where(used, e_of, e_of[jnp.maximum(sb_end[-1] - 1, 0)])
    local = sid - (sb_end - sb_cnt)[e_of]
    nb = jnp.where(used, jnp.clip(blocks[e_of] - local * SUPER_BLOCKS, 0, SUPER_BLOCKS), 0).astype(jnp.int32)
    rows = SUPER_BLOCKS * MOE_BLOCK
    first = jnp.where(used, start[e_of] + local * rows, 0).astype(jnp.int32)
    valid = jnp.where(used, jnp.clip(counts[e_of] - local * rows, 0, rows), 0).astype(jnp.int32)
    tail = jnp.zeros((rows + SMEM_ALIGN,), jnp.int32)
    tok = jnp.concatenate([order // TOP_K, tail])
    dst = jnp.concatenate([(order % TOP_K) * n_tok + order // TOP_K, tail])
    return e_of, nb, first, valid, tok, dst, n_super


def _experts(packed, e_of, nb, first, valid, tok, dst, w_gate, w_up, w_down, n_super, n_assign, *, name):
    n_exp, d, hid = w_gate.shape
    half = d // 2
    rows = SUPER_BLOCKS * MOE_BLOCK
    hc = min(HIDDEN_CHUNK, hid)
    n_hc = hid // hc
    dcw = min(DOWN_CHUNK, half)
    n_dc = d // dcw
    n_lo = half // dcw
    steps = n_hc + n_dc
    sub = 2 * MOE_BLOCK

    def unpack(u):
        return (lax.bitcast_convert_type(u << 16, F32).astype(BF16),
                lax.bitcast_convert_type(u & jnp.uint32(0xFFFF0000), F32).astype(BF16))

    def body(e_ref, nb_ref, first_ref, valid_ref, tok_hbm, dst_hbm, x_hbm, wg_ref, wu_ref, wd_ref, y_hbm,
             tok_s, dst_s, x_scr, h_scr, y_scr, wg_b, wu_b, wd_b, sem_i, sem_g, sem_s):
        s = pl.program_id(0)
        h = pl.program_id(1)
        n_blk = nb_ref[s]
        n_rows = n_blk * MOE_BLOCK
        n_valid = valid_ref[s]
        base = pl.multiple_of((first_ref[s] // SMEM_ALIGN) * SMEM_ALIGN, SMEM_ALIGN)
        off = first_ref[s] - base

        def for_blocks(fn):
            def pair(j, carry):
                fn(pl.multiple_of(j * sub, sub), sub)
                return carry
            lax.fori_loop(0, n_blk // 2, pair, 0)

            @pl.when(n_blk % 2 == 1)
            def _():
                fn(pl.multiple_of((n_blk - 1) * MOE_BLOCK, MOE_BLOCK), MOE_BLOCK)

        def gather_copy(i):
            t = jnp.where(i < n_valid, tok_s[off + i], 0)
            return pltpu.make_async_copy(x_hbm.at[pl.ds(t, 1)], x_scr.at[pl.ds(i, 1)], sem_g)

        def scatter_copy(i):
            t = jnp.where(i < n_valid, dst_s[off + i], n_assign + i)
            return pltpu.make_async_copy(y_scr.at[pl.ds(i, 1)], y_hbm.at[pl.ds(t, 1)], sem_s)

        @pl.when((s == 0) & (h == 0))
        def _():
            y_scr[...] = jnp.zeros(y_scr.shape, jnp.uint32)
            cp = pltpu.make_async_copy(y_scr, y_hbm.at[pl.ds(n_assign, rows)], sem_s)
            cp.start()
            cp.wait()

        @pl.when((n_blk > 0) & (h == 0))
        def _():
            c1 = pltpu.make_async_copy(tok_hbm.at[pl.ds(base, rows + SMEM_ALIGN)], tok_s, sem_i.at[0])
            c2 = pltpu.make_async_copy(dst_hbm.at[pl.ds(base, rows + SMEM_ALIGN)], dst_s, sem_i.at[1])
            c1.start()
            c2.start()
            c1.wait()
            c2.wait()

            def issue(i, carry):
                for u in range(DMA_UNROLL):
                    gather_copy(i * DMA_UNROLL + u).start()
                return carry
            lax.fori_loop(0, n_rows // DMA_UNROLL, issue, 0)

            def drain(i, carry):
                for u in range(DMA_UNROLL):
                    gather_copy(i * DMA_UNROLL + u).wait()
                return carry
            lax.fori_loop(0, n_rows // DMA_UNROLL, drain, 0)

        @pl.when((n_blk > 0) & (h < n_hc))
        def _():
            wg_b[...] = wg_ref[0].astype(BF16)
            wu_b[...] = wu_ref[0].astype(BF16)

            def hidden(r0, m):
                x_lo, x_hi = unpack(x_scr[pl.ds(r0, m), :])
                gate = (jnp.dot(x_lo, wg_b[:half], preferred_element_type=F32)
                        + jnp.dot(x_hi, wg_b[half:], preferred_element_type=F32))
                up = (jnp.dot(x_lo, wu_b[:half], preferred_element_type=F32)
                      + jnp.dot(x_hi, wu_b[half:], preferred_element_type=F32))
                h_scr[h, pl.ds(r0, m), :] = (_silu(gate) * up).astype(BF16)
            for_blocks(hidden)

        def down_phase(k):
            col = (k % n_lo) * dcw

            @pl.when((n_blk > 0) & (h == n_hc + k))
            def _():
                wd_b[...] = wd_ref[0].astype(BF16)

                def down(r0, m):
                    y = jnp.dot(h_scr[0, pl.ds(r0, m), :], wd_b[:hc], preferred_element_type=F32)
                    for c in range(1, n_hc):
                        y = y + jnp.dot(h_scr[c, pl.ds(r0, m), :], wd_b[c * hc:(c + 1) * hc],
                                        preferred_element_type=F32)
                    bits = lax.bitcast_convert_type(y.astype(BF16).astype(F32), jnp.uint32)
                    if k < n_lo:
                        y_scr[pl.ds(r0, m), col:col + dcw] = bits >> 16
                    else:
                        y_scr[pl.ds(r0, m), col:col + dcw] = (y_scr[pl.ds(r0, m), col:col + dcw]
                                                              | (bits & jnp.uint32(0xFFFF0000)))
                for_blocks(down)

        for k in range(n_dc):
            down_phase(k)

        @pl.when((n_blk > 0) & (h == steps - 1))
        def _():
            def issue(i, carry):
                for u in range(DMA_UNROLL):
                    scatter_copy(i * DMA_UNROLL + u).start()
                return carry
            lax.fori_loop(0, n_rows // DMA_UNROLL, issue, 0)

            def drain(i, carry):
                for u in range(DMA_UNROLL):
                    scatter_copy(i * DMA_UNROLL + u).wait()
                return carry
            lax.fori_loop(0, n_rows // DMA_UNROLL, drain, 0)

    any_spec = pl.BlockSpec(memory_space=pltpu.MemorySpace.HBM)
    return pl.pallas_call(
        body,
        out_shape=jax.ShapeDtypeStruct((n_assign + rows, half), jnp.uint32),
        grid_spec=pltpu.PrefetchScalarGridSpec(
            num_scalar_prefetch=4, grid=(n_super, steps),
            in_specs=[any_spec, any_spec, any_spec,
                      pl.BlockSpec((1, d, hc), lambda s, h, e, *_: (e[s], 0, jnp.minimum(h, n_hc - 1))),
                      pl.BlockSpec((1, d, hc), lambda s, h, e, *_: (e[s], 0, jnp.minimum(h, n_hc - 1))),
                      pl.BlockSpec((1, hid, dcw), lambda s, h, e, *_: (e[s], 0, jnp.maximum(h - n_hc, 0)))],
            out_specs=any_spec,
            scratch_shapes=[pltpu.SMEM((rows + SMEM_ALIGN,), jnp.int32), pltpu.SMEM((rows + SMEM_ALIGN,), jnp.int32),
                            pltpu.VMEM((rows, half), jnp.uint32), pltpu.VMEM((n_hc, rows, hc), BF16),
                            pltpu.VMEM((rows, half), jnp.uint32),
                            pltpu.VMEM((d, hc), BF16), pltpu.VMEM((d, hc), BF16), pltpu.VMEM((hid, dcw), BF16),
                            pltpu.SemaphoreType.DMA((2,)), pltpu.SemaphoreType.DMA, pltpu.SemaphoreType.DMA]),
        compiler_params=_params(("arbitrary", "arbitrary"), has_side_effects=True,
                                disable_bounds_checks=True),
        name=name,
    )(e_of, nb, first, valid, tok, dst, packed, w_gate, w_up, w_down)


def _shared_hidden(packed, w_gate, w_up, *, tm, tn, name):
    m, half = packed.shape
    hid = w_gate.shape[1]

    def body(x_ref, wg_ref, wu_ref, o_ref):
        u = x_ref[...]
        x_lo = lax.bitcast_convert_type(u << 16, F32).astype(BF16)
        x_hi = lax.bitcast_convert_type(u & jnp.uint32(0xFFFF0000), F32).astype(BF16)
        gate = _dot(x_lo, wg_ref[:half]) + _dot(x_hi, wg_ref[half:])
        up = _dot(x_lo, wu_ref[:half]) + _dot(x_hi, wu_ref[half:])
        o_ref[...] = (_silu(gate) * up).astype(BF16)

    w_spec = pl.BlockSpec((2 * half, tn), lambda i, j: (0, j))
    return pl.pallas_call(
        body, out_shape=jax.ShapeDtypeStruct((m, hid), BF16), grid=(m // tm, hid // tn),
        in_specs=[pl.BlockSpec((tm, half), lambda i, j: (i, 0)), w_spec, w_spec],
        out_specs=pl.BlockSpec((tm, tn), lambda i, j: (i, j)),
        compiler_params=_params(("parallel", "parallel")), name=name,
    )(packed, w_gate, w_up)


def _rope_tables(pos):
    half = ROPE_DIM // 2
    inv = ROPE_THETA ** (-2.0 * jnp.arange(half, dtype=F32) / ROPE_DIM)
    ang = pos.astype(F32)[:, None] * inv[None, :]
    cos, sin = jnp.cos(ang), jnp.sin(ang)
    n = pos.shape[0]
    rest = HEAD_DIM - ROPE_DIM
    c = jnp.concatenate([cos, cos, jnp.ones((n, rest), F32)], axis=1)
    sa = jnp.concatenate([-sin, jnp.zeros((n, half + rest), F32)], axis=1)
    sb = jnp.concatenate([jnp.zeros((n, half), F32), sin, jnp.zeros((n, rest), F32)], axis=1)
    return c, sa, sb


def _overlap_matrix(nch, nsp):
    c = jnp.arange(nch)[:, None]
    s = jnp.arange(nsp)[None, :]
    c_start, c_end = c * CMP_STRIDE, c * CMP_STRIDE + CMP_LEN - 1
    return ((c_start < s * SEL_LEN + SEL_LEN) & (c_end >= s * SEL_LEN)).astype(F32)


def _decay_operand(f_all, n_groups):
    b_sz, length, heads = f_all.shape
    return (f_all * LOG2E).reshape(b_sz, length, n_groups, heads // n_groups).transpose(0, 2, 3, 1)


def kernel(x_prompt, x_sample, c_prompt, c_sample, cache_fox_k, cache_fox_v, cache_fox_logf, cache_cmp_k, cache_cmp_v, cache_sel_k, cache_sel_v, state_win_k, state_win_v, page_table, w_ada, b_ada, g_mix, g_ffn, w_in, b_forget, g_q_fox, g_k_fox, g_q_nsa, g_k_cmp, g_k_sel, g_k_win, cmp_pe_k, cmp_pe_v, w_cmp_k1, w_cmp_k2, w_cmp_v1, w_cmp_v2, w_branch, w_out, w_router, router_bias, w_exp_gate, w_exp_up, w_exp_down, w_sh_gate, w_sh_up, w_sh_down):
    hd = HEAD_DIM
    bp, seq, d = x_prompt.shape
    bs, dseq, _ = x_sample.shape
    assert bp == 1
    n_pool, page = cache_fox_k.shape[:2]
    past = page_table.shape[1] * page
    w_len = state_win_k.shape[1]
    gf, gn = FOX_KV_HEADS, NSA_KV_HEADS
    assert gf == gn and FOX_HEADS == NSA_HEADS
    n_p, n_s = bp * seq, bs * dseq
    n_all = n_p + n_s
    kvw = gn * hd
    qw = NSA_HEADS * hd
    page_table = page_table.astype(jnp.int32)

    c_all = jnp.concatenate([c_prompt, c_sample], axis=0)
    n_c = c_all.shape[0]
    n_cp = _round_up(n_c, SUBLANES)
    c_all = jnp.pad(c_all, ((0, n_cp - n_c), (0, 0)))
    mod = _matmul(c_all, w_ada, tm=n_cp, tn=_tile(6 * d, 2048, LANES), tk=_tile(d, 512, LANES),
                  out_dtype=F32, name="ada", prologue=_silu,
                  extras=(b_ada.reshape(1, 6 * d),),
                  extra_specs=(((1, _tile(6 * d, 2048, LANES)), lambda i, j, k: (0, j)),),
                  epilogue=lambda r, b: r + b)
    sh1, sc1, ga1, sh2, sc2, ga2 = [mod[:, i * d:(i + 1) * d] for i in range(6)]
    per_row = lambda v: jnp.repeat(v[bp:bp + bs], dseq, axis=0)

    xp2, xs2 = x_prompt.reshape(n_p, d), x_sample.reshape(n_s, d)
    tt_p = _tile(n_p, 256, 16)
    h_p = _norm_mod(xp2, g_mix, sc1[:1], sh1[:1], tt=tt_p, name="norm1_prompt")
    h_s = _norm_mod(xs2, g_mix, per_row(sc1), per_row(sh1), tt=n_s, name="norm1_sample")
    h_all = jnp.concatenate([h_p, h_s], axis=0)

    fw, nw = FOX_HEADS * hd, NSA_HEADS * hd
    kvf = gf * hd
    sizes = [fw, kvf, kvf, FOX_HEADS, nw, kvw, kvw, kvw, kvw, kvw, kvw, 3 * NSA_HEADS, N_BRANCH * d]
    offs = [0]
    for s_ in sizes:
        offs.append(offs[-1] + s_)
    seg = lambda i: w_in[:, offs[i]:offs[i + 1]]
    order = [0, 4, 1, 2, 5, 6, 7, 8, 9, 10, 12, 3, 11]
    assert FOX_HEADS + 3 * NSA_HEADS <= LANES
    np_cols = sum(sizes[i] for i in order[:-2]) + LANES
    tn_in = _tile(_round_up(np_cols, 1280), 1280, LANES) if np_cols > 1280 else np_cols
    np_pad = _round_up(np_cols, tn_in)
    w_in_p = jnp.concatenate([seg(i) for i in order]
                             + [jnp.zeros((d, np_pad - np_cols + LANES - FOX_HEADS - 3 * NSA_HEADS), w_in.dtype)],
                             axis=1).astype(BF16)
    col = {}
    acc_ = 0
    for i in order[:-2]:
        col[i] = acc_
        acc_ += sizes[i]
    col_gates = acc_
    tm_all = _tile(n_all, 1664, 16)
    proj = _matmul(h_all, w_in_p, tm=tm_all, tn=tn_in, tk=_tile(d, 1024, LANES), out_dtype=F32, name="in_proj")

    pos_all = jnp.concatenate([jnp.arange(seq, dtype=jnp.int32)] * bp
                              + [past + jnp.arange(dseq, dtype=jnp.int32)] * bs)
    tabs = _rope_tables(pos_all)
    tr = _tile(n_all, 320, SUBLANES)
    qf = _head_prep(proj, col[0], FOX_HEADS, g_q_fox, None, tr=tr, name="prep_qf")
    kf = _head_prep(proj, col[1], gf, g_k_fox, None, tr=tr, name="prep_kf")
    qn = _head_prep(proj, col[4], NSA_HEADS, g_q_nsa, tabs, tr=tr, name="prep_qn")
    kc = _head_prep(proj, col[5], gn, None, tabs, tr=tr, name="prep_kc")
    ks = _head_prep(proj, col[7], gn, g_k_sel, tabs, tr=tr, name="prep_ks")
    kw = _head_prep(proj, col[9], gn, g_k_win, tabs, tr=tr, name="prep_kw")
    vf = proj[:, col[2]:col[2] + kvf]
    vc = proj[:, col[6]:col[6] + kvw]
    vs = proj[:, col[8]:col[8] + kvw]
    vw = proj[:, col[10]:col[10] + kvw]
    gates = _gates(proj, col_gates, b_forget, tr=tr, name="gates")
    logf = gates[:, :FOX_HEADS]

    ident = lambda nb, n: jnp.arange(nb * n, dtype=jnp.int32).reshape(nb, n)
    pps = PAGES_PER_STEP

    def cmp_weights(pe, w1, w2):
        hid = w1.shape[1]
        halfk = CMP_STRIDE * hd
        assert CMP_LEN == 2 * CMP_STRIDE
        w1ab = jnp.concatenate([w1[:halfk], w1[halfk:]], axis=1).astype(BF16)
        pe_row = jnp.pad(pe.reshape(1, CMP_LEN * hd), ((0, SUBLANES - 1), (0, 0)))
        cpe = _matmul(pe_row, w1, tm=SUBLANES, tn=hid, tk=_tile(CMP_LEN * hd, 1024, LANES),
                      out_dtype=F32, name="cmp_pe")[:1]
        return w1ab, w1[halfk:], cpe, w2.astype(BF16)

    wk = cmp_weights(cmp_pe_k, w_cmp_k1, w_cmp_k2)
    wv = cmp_weights(cmp_pe_v, w_cmp_v1, w_cmp_v2)
    hid_c = w_cmp_k1.shape[1]

    def compress(pages, table, weights, gain, new_rows, name):
        w1ab, w1b, cpe, w2 = weights
        nb = table.shape[0]
        papb = _compress_first(pages, table, w1ab, gn, name=name + "_a")
        if new_rows is None:
            pb_next = jnp.zeros((nb, 1, gn * hid_c), F32)
        else:
            t_new = new_rows.shape[1]
            xn = jnp.pad(new_rows.reshape(nb, t_new, gn, hd), ((0, 0), (0, CMP_STRIDE - t_new), (0, 0), (0, 0)))
            xn = xn.transpose(0, 2, 1, 3).reshape(nb * gn, CMP_STRIDE * hd)
            rows_p = _round_up(nb * gn, SUBLANES)
            xn = jnp.pad(xn, ((0, rows_p - nb * gn), (0, 0)))
            pb_next = _matmul(xn, w1b, tm=rows_p, tn=hid_c, tk=_tile(CMP_STRIDE * hd, 1024, LANES),
                              out_dtype=F32, name=name + "_new")[:nb * gn].reshape(nb, 1, gn * hid_c)
        return _compress_second(papb, pb_next, cpe, w2, gain, gn, name=name + "_b")

    n_pg_p = seq // page
    tab_p = ident(bp, n_pg_p)
    tq_p = _tile(seq, 256, 16)
    tk_p = _tile(seq, 1024, LANES)
    as_pages = lambda a, rows: a[:n_p].reshape(n_p // rows, rows, a.shape[1])
    q_p = lambda a: a[:n_p].reshape(bp, seq, a.shape[1])

    f_p = _cumsum_logf(logf[:n_p].reshape(n_p // page, page * FOX_HEADS // LANES, LANES), tab_p, None,
                       name="cumsum_prompt")
    o_fox_p = _attention(q_p(qf), as_pages(kf, tk_p), as_pages(vf, tk_p), ident(bp, seq // tk_p),
                         tq=tq_p, n_pages=1, q0=0, pos_base=0, fuse_groups=False,
                         decay=_decay_operand(f_p, gf), name="fox_prompt")
    ck_p = compress(kc[:n_p].reshape(n_pg_p, page * gn, hd), tab_p, wk, g_k_cmp, None, "cmpk_prompt")
    cv_p = compress(vc[:n_p].reshape(n_pg_p, page * gn, hd), tab_p, wv, None, None, "cmpv_prompt")
    nch_p = seq // CMP_STRIDE
    ns_p = -(-seq // SEL_LEN)
    nsp_p = _round_up(ns_p, LANES)
    o_cmp_p, mask_p = _cmp_select(q_p(qn), ck_p, cv_p, _overlap_matrix(nch_p, nsp_p), tq=tq_p, q0=0,
                                  nc_valid=nch_p - CMP_LEN // CMP_STRIDE + 1, n_blocks=ns_p, name="cmp_prompt")
    o_sel_p = _attention(q_p(qn), as_pages(ks, tk_p), as_pages(vs, tk_p), ident(bp, seq // tk_p),
                         tq=tq_p, n_pages=1, q0=0, pos_base=0, fuse_groups=False,
                         block_mask=mask_p, name="sel_prompt")
    wtile = min(WINDOW, seq)
    o_win_p = _attention(q_p(qn), as_pages(kw, wtile), as_pages(vw, wtile), ident(bp, seq // wtile),
                         tq=wtile, n_pages=1, q0=0, pos_base=0, fuse_groups=False,
                         window=WINDOW, band_steps=2, name="win_prompt")

    tq_s = _round_up(dseq, SUBLANES)
    q_s = lambda a: jnp.pad(a[n_p:].reshape(bs, dseq, a.shape[1]), ((0, 0), (0, tq_s - dseq), (0, 0)))
    new_page = lambda a, rows: jnp.pad(a[n_p:].reshape(bs, dseq, a.shape[1]), ((0, 0), (0, rows - dseq), (0, 0)))
    rows_pg = lambda c: c.reshape(c.shape[0], -1, hd)
    new_rows = lambda a: rows_pg(new_page(a, page))
    rep_n = NSA_HEADS // gn
    q_dec = lambda a: q_s(a).reshape(bs, tq_s, gn, rep_n, hd).transpose(0, 2, 3, 1, 4).reshape(bs, gn * rep_n * tq_s, hd)
    o_dec = lambda o: o.reshape(bs, gn, rep_n, tq_s, hd).transpose(0, 3, 1, 2, 4).reshape(bs, tq_s, gn * rep_n * hd)

    lf_new = new_page(logf, pps * page).reshape(bs, pps * page * FOX_HEADS // LANES, LANES)
    f_s = _cumsum_logf(cache_fox_logf.astype(F32).reshape(n_pool, page * FOX_HEADS // LANES, LANES),
                       page_table, lf_new, name="cumsum_sample")
    ft_s = _decay_operand(f_s, gf).reshape(bs, FOX_HEADS, -1)
    o_fox_s = o_dec(_attention_dec(q_dec(qf), rows_pg(cache_fox_k), rows_pg(cache_fox_v), page_table,
                                   (new_rows(kf), new_rows(vf)), tq=tq_s, n_groups=gf, n_pages=pps, q0=past,
                                   pos_base=0, extra_pos=past, decay=ft_s, name="fox_sample"))
    kc_new = kc[n_p:].reshape(bs, dseq, kvw)
    vc_new = vc[n_p:].reshape(bs, dseq, kvw)
    ck_s = compress(cache_cmp_k.reshape(n_pool, page * gn, hd), page_table, wk, g_k_cmp, kc_new, "cmpk_sample")
    cv_s = compress(cache_cmp_v.reshape(n_pool, page * gn, hd), page_table, wv, None, vc_new, "cmpv_sample")
    len_s = past + dseq
    nch_s = past // CMP_STRIDE
    n_chunks_s = max(-(-len_s // CMP_STRIDE), CMP_LEN // CMP_STRIDE)
    assert n_chunks_s == nch_s + 1
    ns_s = -(-len_s // SEL_LEN)
    nsp_s = _round_up(max(ns_s, (past + page) // SEL_LEN), LANES)
    o_cmp_s, mask_s = _cmp_select(q_s(qn), ck_s, cv_s, _overlap_matrix(nch_s, nsp_s), tq=tq_s, q0=past,
                                  nc_valid=n_chunks_s - CMP_LEN // CMP_STRIDE + 1, n_blocks=ns_s,
                                  name="cmp_sample")
    bm_s = mask_s.astype(F32).reshape(bs, tq_s, gn, nsp_s).transpose(0, 2, 1, 3).reshape(bs, gn * tq_s, nsp_s)
    o_sel_s = o_dec(_attention_dec(q_dec(qn), rows_pg(cache_sel_k), rows_pg(cache_sel_v), page_table,
                                   (new_rows(ks), new_rows(vs)), tq=tq_s, n_groups=gn, n_pages=pps, q0=past,
                                   pos_base=0, extra_pos=past, block_mask=bm_s, name="sel_sample"))
    wpg = _tile(w_len, page, SUBLANES)
    n_wpg = w_len // wpg
    o_win_s = o_dec(_attention_dec(q_dec(qn), state_win_k.reshape(bs * n_wpg, wpg * gn, hd),
                                   state_win_v.reshape(bs * n_wpg, wpg * gn, hd), ident(bs, n_wpg),
                                   (new_rows(kw), new_rows(vw)), tq=tq_s, n_groups=gn, n_pages=n_wpg, q0=past,
                                   pos_base=past - w_len, extra_pos=past, window=WINDOW, name="win_sample"))

    unpad = lambda a: a[:, :dseq].reshape(n_s, a.shape[2])
    cat = lambda a_p, a_s: jnp.concatenate([a_p.reshape(n_p, -1), unpad(a_s)], axis=0)
    o_nsa = _nsa_combine(cat(o_cmp_p, o_cmp_s), cat(o_sel_p, o_sel_s), cat(o_win_p, o_win_s), gates,
                         tr=_tile(n_all, 320, 16), name="nsa_combine")
    o_fox = cat(o_fox_p, o_fox_s)
    tn_d = _tile(d // 2, 1024, LANES)
    assert col[12] % tn_d == 0
    merged = _merge(o_fox, o_nsa, w_branch, proj, col[12], tm=_tile(n_all, 640, 16), tn=tn_d,
                    tk=_tile(fw, 512, LANES), name="merge")

    resid = lambda r, x_, g_: x_ + g_ * r
    tm_p = _tile(n_p, 1024, 16)
    x1_p = _matmul(merged, w_out, tm=tm_p, tn=tn_d, tk=_tile(d, 1024, LANES), out_dtype=F32, name="out_prompt",
                   m_rows=n_p, extras=(xp2, ga1[:1]),
                   extra_specs=(((tm_p, tn_d), lambda i, j, k: (i, j)),
                                ((1, tn_d), lambda i, j, k: (0, j))), epilogue=resid)
    x1_s = _matmul(merged, w_out, tm=n_s, tn=tn_d, tk=_tile(d, 1024, LANES), out_dtype=F32, name="out_sample",
                   m_rows=n_s, a_row_off=n_p, extras=(xs2, per_row(ga1)),
                   extra_specs=(((n_s, tn_d), lambda i, j, k: (i, j)),
                                ((n_s, tn_d), lambda i, j, k: (i, j))), epilogue=resid)

    n_exp = w_router.shape[1]
    wr_t = w_router.T.astype(F32)
    pk_p, lg_p = _norm_mod(x1_p, g_ffn, sc2[:1], sh2[:1], tt=tt_p, name="norm2_prompt", router_t=wr_t)
    pk_s, lg_s = _norm_mod(x1_s, g_ffn, per_row(sc2), per_row(sh2), tt=n_s, name="norm2_sample", router_t=wr_t)
    packed = jnp.concatenate([pk_p, pk_s], axis=0)
    logits_t = jnp.concatenate([lg_p, lg_s], axis=1)
    n_padr = _round_up(n_all, LANES)
    eidx_t, gw_t = _route(jnp.pad(logits_t, ((0, 0), (0, n_padr - n_all))), router_bias,
                          tn=_tile(n_padr, 640, LANES), name="route")
    eidx_t, gw_t = eidx_t[:, :n_all], gw_t[:, :n_all]
    e_of, nb, first, valid, tok, dst, n_super = _dispatch_tables(eidx_t, n_exp)
    n_assign = n_all * TOP_K
    ya = _experts(packed, e_of, nb, first, valid, tok, dst, w_exp_gate, w_exp_up, w_exp_down, n_super, n_assign,
                  name="experts")
    gw = gw_t.T
    hs = _shared_hidden(packed, w_sh_gate, w_sh_up, tm=_tile(n_all, 640, 16),
                        tn=_tile(w_sh_gate.shape[1], 256, LANES), name="shared_hidden")

    n_cb = d // tn_d
    assert n_cb % 2 == 0

    def final(r, x_, g_, w_, *ys):
        low = pl.program_id(0) < n_cb // 2
        val = lambda u: lax.bitcast_convert_type(jnp.where(low, u << 16, u & jnp.uint32(0xFFFF0000)), F32)
        routed = val(ys[0]) * w_[:, 0:1]
        for k_ in range(1, TOP_K):
            routed = routed + val(ys[k_]) * w_[:, k_:k_ + 1]
        return x_ + g_ * (r + routed)

    w_sd = w_sh_down.astype(BF16)

    def final_call(m_rows, row_off, tm, x1, gate, gate_rows, name):
        assert n_all % tm == 0 and row_off % tm == 0
        ro, per_k = row_off // tm, n_all // tm
        g_spec = ((tm, tn_d), lambda i, j, k: (i, j)) if gate_rows else ((1, tn_d), lambda i, j, k: (0, j))
        y_specs = tuple(((tm, tn_d), functools.partial(lambda i, j, k, kk: (kk * per_k + ro + i, j % (n_cb // 2)), kk=kk))
                        for kk in range(TOP_K))
        return _matmul(hs, w_sd, tm=tm, tn=tn_d, tk=_tile(w_sd.shape[0], 1024, LANES), out_dtype=F32,
                       name=name, m_rows=m_rows, a_row_off=row_off, j_outer=True,
                       extras=(x1, gate, gw) + (ya,) * TOP_K,
                       extra_specs=(((tm, tn_d), lambda i, j, k: (i, j)), g_spec,
                                    ((tm, TOP_K), lambda i, j, k: (i + ro, 0))) + y_specs,
                       epilogue=final)

    tm_f = _tile(n_s, 128, 16)
    assert n_p % tm_f == 0
    y_p = final_call(n_p, 0, tm_f, x1_p, ga2[:1], False, "final_prompt")
    y_s = final_call(n_s, n_p, tm_f, x1_s, per_row(ga2), True, "final_sample")

    heads4 = lambda a, lo, hi, b_, t_, g_: a[lo:hi].reshape(b_, t_, g_, hd)
    wl_p = min(WINDOW, seq)
    prompt_state = (heads4(kf, 0, n_p, bp, seq, gf), heads4(vf, 0, n_p, bp, seq, gf),
                    logf[:n_p].reshape(bp, seq, FOX_HEADS),
                    heads4(kc, 0, n_p, bp, seq, gn), heads4(vc, 0, n_p, bp, seq, gn),
                    heads4(ks, 0, n_p, bp, seq, gn), heads4(vs, 0, n_p, bp, seq, gn),
                    heads4(kw, 0, n_p, bp, seq, gn)[:, -wl_p:], heads4(vw, 0, n_p, bp, seq, gn)[:, -wl_p:])
    s4 = lambda a, g_: heads4(a, n_p, n_all, bs, dseq, g_)
    wl_s = min(WINDOW, w_len + dseq)
    win_cat = lambda st, new: jnp.concatenate([st, new.astype(st.dtype)], axis=1)[:, -wl_s:]
    sample_state = (s4(kf, gf), s4(vf, gf), logf[n_p:].reshape(bs, dseq, FOX_HEADS),
                    s4(kc, gn), s4(vc, gn), s4(ks, gn), s4(vs, gn),
                    win_cat(state_win_k, s4(kw, gn)), win_cat(state_win_v, s4(vw, gn)))
    return (y_p.reshape(bp, seq, d), y_s.reshape(bs, dseq, d)) + prompt_state + sample_state
```
